```python
import jax, jax.numpy as jnp
from jax import lax
import numpy as np

D_MODEL = 2048
BATCH = 4
SEQ = 4096
DEPTH = 2

GRID_W = 64
CTX_LEN = 256
N_BRANCH = 3
BRANCH_W = D_MODEL // 2
RET_DK = 128
RET_DV = 128
RET_HEADS = BRANCH_W // RET_DK
RET_CHUNK = 128
POOL_WINDOWS = (2, 4, 8, 16)
POOL_GROUP = BRANCH_W // len(POOL_WINDOWS)
SGU_CHUNK = 128
SGU_GROUPS = 8
SGU_GW = BRANCH_W // SGU_GROUPS
N_EXPERTS = 32
N_EXPERT_GROUPS = 4
TOP_K = 2
D_FF_EXPERT = D_MODEL // 2
MOE_BLOCK = 256
ROPE_BASE = 10000.0
NORM_EPS = 1e-6
D_IN = 7 * BRANCH_W + N_BRANCH * D_MODEL

kernel_name = "hybrid_retention_pool_sgu_moe_trunk"


def rms_norm(x, g):
    xf = x.astype(jnp.float32)
    r = lax.rsqrt(jnp.mean(xf * xf, axis=-1, keepdims=True) + NORM_EPS)
    return (xf * r).astype(x.dtype) * g


def modulate(x, g, shift, scale):
    return rms_norm(x, g) * (1 + scale) + shift


def rope_tables(n):
    t = jnp.arange(n)
    rows = (t // GRID_W).astype(jnp.float32)
    cols = (t % GRID_W).astype(jnp.float32)
    quarter = RET_DK // 4
    inv = ROPE_BASE ** (-jnp.arange(quarter, dtype=jnp.float32) / quarter)
    ang = jnp.concatenate([rows[:, None] * inv, cols[:, None] * inv], axis=-1)
    return jnp.cos(ang), jnp.sin(ang)


def apply_rope(t, cos, sin):
    half = t.shape[-1] // 2
    t1 = t[..., :half].astype(jnp.float32)
    t2 = t[..., half:].astype(jnp.float32)
    return jnp.concatenate([t1 * cos - t2 * sin, t1 * sin + t2 * cos], axis=-1).astype(t.dtype)


def split_heads(t):
    b, n, _ = t.shape
    return t.reshape(b, n, RET_HEADS, -1).transpose(0, 2, 1, 3)


def retention_scan(q, k, v, log_gamma, s0):
    b, h, n, dk = q.shape
    dv = v.shape[-1]
    nc = n // RET_CHUNK
    qc = q.reshape(b, h, nc, RET_CHUNK, dk)
    kc = k.reshape(b, h, nc, RET_CHUNK, dk)
    vc = v.reshape(b, h, nc, RET_CHUNK, dv)
    pos = jnp.arange(RET_CHUNK, dtype=jnp.float32)
    diff = pos[:, None] - pos[None, :]
    lg = log_gamma[:, None, None]
    decay_in = jnp.where(diff >= 0, jnp.exp(jnp.maximum(diff, 0.0) * lg), 0.0)
    scores = jnp.einsum('bhncd,bhnmd->bhncm', qc, kc) * decay_in[None, :, None]
    inner = jnp.einsum('bhncm,bhnme->bhnce', scores, vc.astype(jnp.float32))
    to_end = jnp.exp((RET_CHUNK - 1 - pos)[None, :] * log_gamma[:, None])
    kv = jnp.einsum('bhncd,bhnce,hc->nbhde', kc.astype(jnp.float32), vc.astype(jnp.float32), to_end)
    chunk_decay = jnp.exp(RET_CHUNK * log_gamma)[None, :, None, None]

    def step(s, kv_n):
        return chunk_decay * s + kv_n, s

    s_final, s_before = lax.scan(step, s0, kv)
    from_start = jnp.exp((pos + 1)[None, :] * log_gamma[:, None])
    cross = jnp.einsum('bhncd,nbhde,hc->bhnce', qc.astype(jnp.float32), s_before, from_start)
    return (inner + cross).reshape(b, h, n, dv), s_final


def retention_out(y, g):
    b, h, n, dv = y.shape
    mu = jnp.mean(y, axis=-1, keepdims=True)
    var = jnp.mean(jnp.square(y - mu), axis=-1, keepdims=True)
    yn = ((y - mu) * lax.rsqrt(var + NORM_EPS)).transpose(0, 2, 1, 3).reshape(b, n, h * dv)
    return yn.astype(g.dtype) * jax.nn.silu(g)


def window_mean(x, w, axis):
    n = x.shape[axis]
    pad = [(0, 0)] * x.ndim
    pad[axis] = (1, 0)
    cs = jnp.pad(jnp.cumsum(x.astype(jnp.float32), axis=axis), pad)
    t = jnp.arange(n)
    lo = jnp.clip(t - w // 2, 0, n)
    hi = jnp.clip(t + w // 2, 0, n)
    total = jnp.take(cs, hi, axis=axis) - jnp.take(cs, lo, axis=axis)
    shape = [1] * x.ndim
    shape[axis] = n
    return total / (hi - lo).astype(jnp.float32).reshape(shape)


def pool_mix(xp, w_pool, scale, grid_rows):
    b, n, _ = xp.shape
    outs = []
    for xg, w in zip(jnp.split(xp, len(POOL_WINDOWS), axis=-1), POOL_WINDOWS):
        if grid_rows is None:
            m = window_mean(xg, w, 1)
        else:
            g2 = xg.reshape(b, grid_rows, GRID_W, POOL_GROUP)
            m = window_mean(window_mean(g2, w, 2), w, 1).reshape(b, n, POOL_GROUP)
        outs.append(m.astype(xp.dtype) - xg)
    d = jnp.stack(outs, axis=2)
    y = jnp.einsum('bngc,gcd->bngd', d, w_pool).reshape(b, n, BRANCH_W)
    return y * scale


def spatial_gate(u, v, w_s, b_s):
    b, n, _ = u.shape
    vf = v.astype(jnp.float32)
    mu = jnp.mean(vf, axis=-1, keepdims=True)
    var = jnp.mean(jnp.square(vf - mu), axis=-1, keepdims=True)
    vn = ((vf - mu) * lax.rsqrt(var + NORM_EPS)).astype(v.dtype)
    vg = vn.reshape(b, n // SGU_CHUNK, SGU_CHUNK, SGU_GROUPS, SGU_GW)
    mixed = jnp.einsum('gpq,bkqgc->bkpgc', w_s, vg) + b_s.T[:, :, None]
    return u * mixed.reshape(b, n, BRANCH_W)


def merge_branches(branches, gate_logits, w_branch, w_out):
    gates = jnp.split(gate_logits, N_BRANCH, axis=-1)
    merged = jax.nn.sigmoid(gates[0]) * (branches[0] @ w_branch[0])
    for i in range(1, N_BRANCH):
        merged = merged + jax.nn.sigmoid(gates[i]) * (branches[i] @ w_branch[i])
    return merged @ w_out


def token_mixers(pc, px, grid_rows, rope_cos, rope_sin, decay, pool_w, pool_scale, sgu_w, sgu_b,
                 w_branch, w_out, want_ctx):
    cuts = [BRANCH_W * i for i in range(1, 8)]
    qc, kc, vc, gc, poolc, uc, svc, gatec = jnp.split(pc, cuts, axis=-1)
    qx, kx, vx, gx, poolx, ux, svx, gatex = jnp.split(px, cuts, axis=-1)
    b = px.shape[0]
    lg = jax.nn.log_sigmoid(decay.astype(jnp.float32))
    qscale = RET_DK ** -0.5
    qx = apply_rope(split_heads(qx), rope_cos, rope_sin) * qscale
    kx = apply_rope(split_heads(kx), rope_cos, rope_sin)
    vx = split_heads(vx)
    qc = split_heads(qc) * qscale
    kc = split_heads(kc)
    vc = split_heads(vc)
    flip = lambda t: jnp.flip(t, axis=2)
    s0 = jnp.zeros((b, RET_HEADS, RET_DK, RET_DV), jnp.float32)
    yc_f, s_f = retention_scan(qc, kc, vc, lg[0], s0)
    yc_b, s_b = retention_scan(flip(qc), flip(kc), flip(vc), lg[1], s0)
    yx_f, _ = retention_scan(qx, kx, vx, lg[0], s_f)
    yx_b, _ = retention_scan(flip(qx), flip(kx), flip(vx), lg[1], s_b)
    ret_x = retention_out(yx_f + flip(yx_b), gx)
    out_x = merge_branches(
        (ret_x, pool_mix(poolx, pool_w, pool_scale, grid_rows), spatial_gate(ux, svx, sgu_w, sgu_b)),
        gatex, w_branch, w_out)
    if not want_ctx:
        return out_x, None
    ret_c = retention_out(yc_f + flip(yc_b), gc)
    out_c = merge_branches(
        (ret_c, pool_mix(poolc, pool_w, pool_scale, None), spatial_gate(uc, svc, sgu_w, sgu_b)),
        gatec, w_branch, w_out)
    return out_x, out_c


def moe(h, w_router, router_bias, w_gu, w_down):
    t_tok, d = h.shape
    per = N_EXPERTS // N_EXPERT_GROUPS
    logits = (h @ w_router).astype(jnp.float32) + router_bias.astype(jnp.float32)
    probs = jax.nn.softmax(logits, axis=-1)
    grouped = probs.reshape(t_tok, N_EXPERT_GROUPS, per)
    group_score = lax.top_k(grouped, TOP_K)[0].sum(-1)
    best = jnp.argmax(group_score, axis=-1)
    in_group = jnp.take_along_axis(grouped, best[:, None, None], axis=1)[:, 0]
    top_p, top_i = lax.top_k(in_group, TOP_K)
    expert = best[:, None] * per + top_i
    gate = top_p / jnp.sum(top_p, axis=-1, keepdims=True)
    tk = t_tok * TOP_K
    flat_e = expert.reshape(-1)
    order = jnp.argsort(flat_e)
    sorted_e = flat_e[order]
    token_of = order // TOP_K
    counts = jnp.zeros((N_EXPERTS,), jnp.int32).at[flat_e].add(1)
    padded = ((counts + MOE_BLOCK - 1) // MOE_BLOCK) * MOE_BLOCK
    start_sorted = jnp.cumsum(counts) - counts
    ends_pad = jnp.cumsum(padded)
    start_pad = ends_pad - padded
    dest = start_pad[sorted_e] + (jnp.arange(tk, dtype=jnp.int32) - start_sorted[sorted_e])
    n_blocks = -(-(tk + N_EXPERTS * (MOE_BLOCK - 1)) // MOE_BLOCK)
    rows = n_blocks * MOE_BLOCK
    buf = jnp.zeros((rows, d), h.dtype).at[dest].set(h[token_of])
    block_start = jnp.arange(n_blocks, dtype=jnp.int32) * MOE_BLOCK
    block_e = jnp.minimum(jnp.searchsorted(ends_pad, block_start, side='right'), N_EXPERTS - 1)

    def expert_block(args):
        xb, e = args
        a, g = jnp.split(xb @ w_gu[e], 2, axis=-1)
        return (jax.nn.silu(a) * g) @ w_down[e]

    ys = lax.map(expert_block, (buf.reshape(n_blocks, MOE_BLOCK, d), block_e)).reshape(rows, d)
    y_slot = ys[dest].astype(jnp.float32) * gate.reshape(-1)[order][:, None]
    out = jnp.zeros((t_tok, d), jnp.float32).at[token_of].add(y_slot)
    return out.astype(h.dtype)


def setup_inputs(seed: int = 0) -> dict:
    key = jax.random.key(seed)
    ks = jax.random.split(key, 21)
    f32 = jnp.float32
    nrm = lambda k, shape, s: jax.random.normal(k, shape, f32) * s
    heads = jnp.arange(RET_HEADS, dtype=f32)
    decay_init = jnp.log(2.0 ** (5.0 + heads) - 1.0)
    return {
        "x": nrm(ks[0], (BATCH, SEQ, D_MODEL), 1.0),
        "c": nrm(ks[1], (BATCH, D_MODEL), 1.0),
        "ctx": nrm(ks[2], (BATCH, CTX_LEN, D_MODEL), 1.0),
        "c_ctx": nrm(ks[3], (D_MODEL,), 1.0),
        "w_ada": nrm(ks[4], (DEPTH, D_MODEL, 6 * D_MODEL), 0.5 * D_MODEL ** -0.5),
        "b_ada": nrm(ks[5], (DEPTH, 6 * D_MODEL), 0.01),
        "norm1_g": 1.0 + nrm(ks[6], (DEPTH, D_MODEL), 0.02),
        "norm2_g": 1.0 + nrm(ks[7], (DEPTH, D_MODEL), 0.02),
        "w_in": nrm(ks[8], (DEPTH, D_MODEL, D_IN), D_MODEL ** -0.5),
        "ret_decay": decay_init[None, None, :] + nrm(ks[9], (DEPTH, 2, RET_HEADS), 0.01),
        "pool_w": nrm(ks[10], (DEPTH, len(POOL_WINDOWS), POOL_GROUP, POOL_GROUP), POOL_GROUP ** -0.5),
        "pool_scale": 1.0 + nrm(ks[11], (DEPTH, BRANCH_W), 0.1),
        "sgu_w": nrm(ks[12], (DEPTH, SGU_GROUPS, SGU_CHUNK, SGU_CHUNK), SGU_CHUNK ** -0.5),
        "sgu_b": 1.0 + nrm(ks[13], (DEPTH, SGU_GROUPS, SGU_CHUNK), 0.02),
        "w_branch": nrm(ks[14], (DEPTH, N_BRANCH, BRANCH_W, D_MODEL), BRANCH_W ** -0.5),
        "w_out": nrm(ks[15], (DEPTH, D_MODEL, D_MODEL), D_MODEL ** -0.5),
        "w_router": nrm(ks[16], (D_MODEL, N_EXPERTS), D_MODEL ** -0.5),
        "router_bias": nrm(ks[17], (N_EXPERTS,), 0.01),
        "w_gu": nrm(ks[18], (DEPTH, N_EXPERTS, D_MODEL, 2 * D_FF_EXPERT), D_MODEL ** -0.5),
        "w_down": nrm(ks[19], (DEPTH, N_EXPERTS, D_FF_EXPERT, D_MODEL), D_FF_EXPERT ** -0.5),
        "final_g": 1.0 + nrm(ks[20], (D_MODEL,), 0.02),
    }


def reference(x, c, ctx, c_ctx, w_ada, b_ada, norm1_g, norm2_g, w_in, ret_decay, pool_w, pool_scale,
              sgu_w, sgu_b, w_branch, w_out, w_router, router_bias, w_gu, w_down, final_g):
    b, n, d = x.shape
    n_ctx = ctx.shape[1]
    grid_rows = n // GRID_W
    rope_cos, rope_sin = rope_tables(n)
    c_act = jax.nn.silu(c)
    cc_act = jax.nn.silu(c_ctx)
    for l in range(DEPTH):
        want_ctx = l < DEPTH - 1
        sh1, sc1, g1, sh2, sc2, g2 = jnp.split((c_act @ w_ada[l] + b_ada[l])[:, None, :], 6, axis=-1)
        csh1, csc1, cg1, csh2, csc2, cg2 = jnp.split(cc_act @ w_ada[l] + b_ada[l], 6, axis=-1)
        hx = modulate(x, norm1_g[l], sh1, sc1)
        hc = modulate(ctx, norm1_g[l], csh1, csc1)
        p = jnp.concatenate([hc, hx], axis=1) @ w_in[l]
        mix_x, mix_c = token_mixers(p[:, :n_ctx], p[:, n_ctx:], grid_rows, rope_cos, rope_sin,
                                    ret_decay[l], pool_w[l], pool_scale[l], sgu_w[l], sgu_b[l],
                                    w_branch[l], w_out[l], want_ctx)
        x = x + g1 * mix_x
        hx2 = modulate(x, norm2_g[l], sh2, sc2)
        if want_ctx:
            ctx = ctx + cg1 * mix_c
            hc2 = modulate(ctx, norm2_g[l], csh2, csc2)
            tokens = jnp.concatenate([hc2.reshape(-1, d), hx2.reshape(-1, d)], axis=0)
            f = moe(tokens, w_router, router_bias, w_gu[l], w_down[l])
            ctx = ctx + cg2 * f[:b * n_ctx].reshape(b, n_ctx, d)
            x = x + g2 * f[b * n_ctx:].reshape(b, n, d)
        else:
            x = x + g2 * moe(hx2.reshape(-1, d), w_router, router_bias, w_gu[l], w_down[l]).reshape(b, n, d)
    return rms_norm(x, final_g)
```

```python
import functools

import numpy as np
import jax
import jax.numpy as jnp
from jax import lax
from jax.experimental import pallas as pl
from jax.experimental.pallas import tpu as pltpu

F32 = jnp.float32
BF16 = jnp.bfloat16

GRID_W = 64
RET_DK = 128
POOL_WINDOWS = (2, 4, 8, 16)
SGU_CHUNK = 128
N_EXPERT_GROUPS = 4
TOP_K = 2
ROPE_BASE = 10000.0
NORM_EPS = 1e-6

V7X_VMEM_LIMIT_BYTES = 56 * 1024 * 1024
LANES = 128

ROW_TILE = 256
RET_CHUNK = 256
POOL_HALO = 512


def _tile(pref, *sizes):
    t = pref
    while any(s % t for s in sizes):
        t //= 2
    return t


def _params(*sem):
    return pltpu.CompilerParams(dimension_semantics=sem, vmem_limit_bytes=V7X_VMEM_LIMIT_BYTES)


def _silu(x):
    return x * jax.nn.sigmoid(x)


def _ada_kernel(c_ref, w_ref, b_ref, o_ref):
    a = _silu(c_ref[...]).astype(BF16)
    o_ref[...] = jnp.dot(a, w_ref[...].astype(BF16), preferred_element_type=F32) + b_ref[...]


def _ada(c_all, w_ada, b_ada, tn=1024):
    depth, d, n = w_ada.shape
    rows = c_all.shape[0]
    return pl.pallas_call(
        _ada_kernel,
        out_shape=jax.ShapeDtypeStruct((depth, rows, n), F32),
        grid=(depth, n // tn),
        in_specs=[
            pl.BlockSpec((rows, d), lambda l, j: (0, 0)),
            pl.BlockSpec((None, d, tn), lambda l, j: (l, 0, j)),
            pl.BlockSpec((None, 1, tn), lambda l, j: (l, 0, j)),
        ],
        out_specs=pl.BlockSpec((None, rows, tn), lambda l, j: (l, 0, j)),
        compiler_params=_params("arbitrary", "arbitrary"),
        name="ada_proj",
    )(c_all, w_ada, b_ada.reshape(depth, 1, n))


def _modulate(x, g, shift, scale):
    r = lax.rsqrt(jnp.mean(x * x, axis=-1, keepdims=True) + NORM_EPS)
    return (x * r) * g * (1.0 + scale) + shift


def _norm_kernel(x_ref, g_ref, sh_ref, sc_ref, o_ref):
    o_ref[...] = _modulate(x_ref[...], g_ref[...], sh_ref[0], sc_ref[0]).astype(o_ref.dtype)


def _mod_index(tiles_per_batch, n_batch):
    return lambda i: (jnp.minimum(i // tiles_per_batch, n_batch), 0, 0)


def _norm(x_tok, g, shift, scale, n_rows, seq):
    t, d = x_tok.shape
    n_batch = shift.shape[0] - 1
    mod = _mod_index(seq // ROW_TILE, n_batch)
    return pl.pallas_call(
        _norm_kernel,
        out_shape=jax.ShapeDtypeStruct((n_rows, d), BF16),
        grid=(n_rows // ROW_TILE,),
        in_specs=[
            pl.BlockSpec((ROW_TILE, d), lambda i: (i, 0)),
            pl.BlockSpec((1, d), lambda i: (0, 0)),
            pl.BlockSpec((1, 1, d), mod),
            pl.BlockSpec((1, 1, d), mod),
        ],
        out_specs=pl.BlockSpec((ROW_TILE, d), lambda i: (i, 0)),
        compiler_params=_params("arbitrary"),
        name="mod_norm",
    )(x_tok, g.reshape(1, d), shift, scale)


def _mm_kernel(a_ref, b_ref, o_ref):
    o_ref[...] = jnp.dot(a_ref[...], b_ref[...], preferred_element_type=F32).astype(o_ref.dtype)


def _matmul(a, b, layer, n_rows, tm, tn, out_dtype=BF16, name="matmul"):
    k = a.shape[1]
    n = b.shape[2]
    return pl.pallas_call(
        _mm_kernel,
        out_shape=jax.ShapeDtypeStruct((n_rows, n), out_dtype),
        grid=(n_rows // tm, n // tn),
        in_specs=[
            pl.BlockSpec((tm, k), lambda i, j: (i, 0)),
            pl.BlockSpec((None, k, tn), lambda i, j: (layer, 0, j)),
        ],
        out_specs=pl.BlockSpec((tm, tn), lambda i, j: (i, j)),
        compiler_params=_params("arbitrary", "arbitrary"),
        name=name,
    )(a, b)


def _mm_residual_kernel(a_ref, b_ref, x_ref, g_ref, o_ref):
    y = jnp.dot(a_ref[...], b_ref[...], preferred_element_type=F32)
    o_ref[...] = x_ref[...] + g_ref[0] * y


def _matmul_residual(a, b, layer, x_tok, gate, n_rows, seq, tn=1024):
    k = a.shape[1]
    n = b.shape[2]
    n_batch = gate.shape[0] - 1
    tm = _tile(1024, n_rows, seq)
    tiles_per_batch = seq // tm
    return pl.pallas_call(
        _mm_residual_kernel,
        out_shape=jax.ShapeDtypeStruct((n_rows, n), F32),
        grid=(n_rows // tm, n // tn),
        in_specs=[
            pl.BlockSpec((tm, k), lambda i, j: (i, 0)),
            pl.BlockSpec((None, k, tn), lambda i, j: (layer, 0, j)),
            pl.BlockSpec((tm, tn), lambda i, j: (i, j)),
            pl.BlockSpec((1, 1, tn), lambda i, j: (jnp.minimum(i // tiles_per_batch, n_batch), 0, j)),
        ],
        out_specs=pl.BlockSpec((tm, tn), lambda i, j: (i, j)),
        compiler_params=_params("arbitrary", "arbitrary"),
        name="out_proj_residual",
    )(a, b, x_tok, gate)


def _merge_kernel(a0_ref, a1_ref, a2_ref, b0_ref, b1_ref, b2_ref, g0_ref, g1_ref, g2_ref, o_ref):
    acc = None
    for a_ref, b_ref, g_ref in ((a0_ref, b0_ref, g0_ref), (a1_ref, b1_ref, g1_ref), (a2_ref, b2_ref, g2_ref)):
        y = jax.nn.sigmoid(g_ref[...].astype(F32)) * jnp.dot(a_ref[...], b_ref[...], preferred_element_type=F32)
        acc = y if acc is None else acc + y
    o_ref[...] = acc.astype(o_ref.dtype)


def _merge(branches, w_branch, layer, p, gate_col0, n_rows, tn=512):
    tm = _tile(1024, n_rows)
    kb = branches[0].shape[1]
    n = w_branch.shape[3]
    a_specs = [pl.BlockSpec((tm, kb), lambda i, j: (i, 0)) for _ in range(3)]
    b_specs = [pl.BlockSpec((None, None, kb, tn), functools.partial(lambda i, j, s: (layer, s, 0, j), s=s))
               for s in range(3)]
    g_specs = [
        pl.BlockSpec((tm, tn), functools.partial(lambda i, j, c: (i, c + j), c=(gate_col0 + s * n) // tn))
        for s in range(3)
    ]
    return pl.pallas_call(
        _merge_kernel,
        out_shape=jax.ShapeDtypeStruct((n_rows, n), BF16),
        grid=(n_rows // tm, n // tn),
        in_specs=a_specs + b_specs + g_specs,
        out_specs=pl.BlockSpec((tm, tn), lambda i, j: (i, j)),
        compiler_params=_params("arbitrary", "arbitrary"),
        name="branch_merge",
    )(*branches, w_branch, w_branch, w_branch, p, p, p)


def _retention_kernel(q_ref, k_ref, v_ref, g_ref, cos_ref, sin_ref, lg_ref, o_ref,
                      sf_all, s_run, dmat, rw, *, n_heads, nc_ctx, nc_x, qscale):
    pss = pl.program_id(1)
    j = pl.program_id(2)
    c = RET_CHUNK
    dk = RET_DK

    @pl.when(jnp.logical_and(pss == 0, j == 0))
    def _tables():
        row = lax.broadcasted_iota(jnp.int32, (c, c), 0)
        col = lax.broadcasted_iota(jnp.int32, (c, c), 1)
        diff = (row - col).astype(F32)
        pos = lax.broadcasted_iota(jnp.int32, (c, dk), 0).astype(F32)
        for h in range(n_heads):
            lgf = lg_ref[h, 0:1, :]
            lgb = lg_ref[h, 1:2, :]
            fwd = jnp.where(diff >= 0, jnp.exp(jnp.maximum(diff, 0.0) * lgf), 0.0)
            bwd = jnp.where(diff <= 0, jnp.exp(jnp.maximum(-diff, 0.0) * lgb), 0.0)
            dmat[h] = fwd + bwd
            lgf1 = lgf[:, :dk]
            lgb1 = lgb[:, :dk]
            rw[h, 0] = jnp.exp((pos + 1.0) * lgf1)
            rw[h, 1] = jnp.exp((c - pos) * lgb1)
            rw[h, 2] = jnp.exp((c - 1.0 - pos) * lgf1)
            rw[h, 3] = jnp.exp(pos * lgb1)

    @pl.when(j == 0)
    def _reset():
        s_run[...] = jnp.zeros_like(s_run)

    cosv = cos_ref[...]
    sinv = sin_ref[...]

    def rope(t):
        return t * cosv + pltpu.roll(t, dk // 2, 1) * sinv

    def state_update(h, kk, vv, direction):
        wv = (vv.astype(F32) * rw[h, 2 + direction]).astype(BF16)
        kv = lax.dot_general(kk, wv, (((0,), (0,)), ((), ())), preferred_element_type=F32)
        decay = jnp.exp(float(c) * lg_ref[h, direction:direction + 1, 0:dk])
        s_run[h] = decay * s_run[h] + kv

    @pl.when(pss == 0)
    def _forward():
        for h in range(n_heads):
            cs = slice(h * dk, (h + 1) * dk)
            sf_all[j, h] = s_run[h]
            kk = rope(k_ref[:, cs].astype(F32)).astype(BF16)
            state_update(h, kk, v_ref[:, cs], 0)

    @pl.when(pss == 1)
    def _backward():
        idx = jnp.where(j < nc_ctx, nc_ctx - 1 - j, nc_ctx + nc_x - 1 - (j - nc_ctx))
        for h in range(n_heads):
            cs = slice(h * dk, (h + 1) * dk)
            kk = rope(k_ref[:, cs].astype(F32)).astype(BF16)
            qq = (rope(q_ref[:, cs].astype(F32)) * qscale).astype(BF16)
            vv = v_ref[:, cs]
            scores = lax.dot_general(qq, kk, (((1,), (1,)), ((), ())), preferred_element_type=F32)
            inner = jnp.dot((scores * dmat[h]).astype(BF16), vv, preferred_element_type=F32)
            states = jnp.concatenate([sf_all[idx, h], s_run[h]], axis=1).astype(BF16)
            cross = jnp.dot(qq, states, preferred_element_type=F32)
            y = inner + rw[h, 0] * cross[:, :dk] + rw[h, 1] * cross[:, dk:]
            mu = jnp.mean(y, axis=-1, keepdims=True)
            yc = y - mu
            var = jnp.mean(yc * yc, axis=-1, keepdims=True)
            yn = yc * lax.rsqrt(var + NORM_EPS)
            o_ref[:, cs] = (yn * _silu(g_ref[:, cs].astype(F32))).astype(o_ref.dtype)
            state_update(h, kk, vv, 1)


def _retention(p, cos_t, sin_t, lg, n_batch, seq, ctx_len):
    bw = lg.shape[0] * RET_DK
    n_heads = lg.shape[0]
    c = RET_CHUNK
    nc_x = seq // c
    nc_ctx = ctx_len // c
    nc = nc_x + nc_ctx
    t = n_batch * (seq + ctx_len)

    def fwd_idx(pss, j):
        bwd = jnp.where(j < nc_ctx, nc_ctx - 1 - j, nc_ctx + nc_x - 1 - (j - nc_ctx))
        return jnp.where(pss == 0, j, bwd)

    def row_block(b, idx):
        return jnp.where(idx < nc_ctx, n_batch * nc_x + b * nc_ctx + idx, b * nc_x + (idx - nc_ctx))

    def kv_map(col):
        return lambda b, pss, j: (row_block(b, fwd_idx(pss, j)), col)

    def qg_map(col):
        return lambda b, pss, j: (row_block(b, fwd_idx(1, jnp.where(pss == 0, 0, j))), col)

    tab_map = lambda b, pss, j: (fwd_idx(pss, j), 0)
    kern = functools.partial(_retention_kernel, n_heads=n_heads, nc_ctx=nc_ctx, nc_x=nc_x,
                             qscale=float(RET_DK) ** -0.5)
    return pl.pallas_call(
        kern,
        out_shape=jax.ShapeDtypeStruct((t, bw), BF16),
        grid=(n_batch, 2, nc),
        in_specs=[
            pl.BlockSpec((c, bw), qg_map(0)),
            pl.BlockSpec((c, bw), kv_map(1)),
            pl.BlockSpec((c, bw), kv_map(2)),
            pl.BlockSpec((c, bw), qg_map(3)),
            pl.BlockSpec((c, RET_DK), tab_map),
            pl.BlockSpec((c, RET_DK), tab_map),
            pl.BlockSpec((n_heads, 2, c), lambda b, pss, j: (0, 0, 0)),
        ],
        out_specs=pl.BlockSpec((c, bw), qg_map(0)),
        scratch_shapes=[
            pltpu.VMEM((nc, n_heads, RET_DK, RET_DK), F32),
            pltpu.VMEM((n_heads, RET_DK, RET_DK), F32),
            pltpu.VMEM((n_heads, c, c), F32),
            pltpu.VMEM((n_heads, 4, c, RET_DK), F32),
        ],
        compiler_params=_params("arbitrary", "arbitrary", "arbitrary"),
        name="retention",
    )(p, p, p, p, cos_t, sin_t, lg)


def _rope_tables(seq, ctx_len):
    t = np.arange(seq)
    rows = (t // GRID_W).astype(np.float32)
    cols = (t % GRID_W).astype(np.float32)
    quarter = RET_DK // 4
    inv = jnp.asarray(ROPE_BASE, F32) ** (-jnp.arange(quarter, dtype=F32) / quarter)
    ang = jnp.concatenate([jnp.asarray(rows)[:, None] * inv, jnp.asarray(cols)[:, None] * inv], axis=-1)
    cos, sin = jnp.cos(ang), jnp.sin(ang)
    cos_t = jnp.concatenate([jnp.ones((ctx_len, RET_DK), F32), jnp.concatenate([cos, cos], axis=-1)], axis=0)
    sin_t = jnp.concatenate([jnp.zeros((ctx_len, RET_DK), F32), jnp.concatenate([-sin, sin], axis=-1)], axis=0)
    return cos_t, sin_t


def _pool_kernel(x_ref, win_ref, w_ref, sc_ref, o_ref, xpad, *, n_tok, halo, grid_rows):
    half = jnp.left_shift(1, pl.program_id(1))
    slab = ROW_TILE
    width = win_ref.shape[1]
    if halo:
        xpad[0:halo, :] = jnp.zeros((halo, xpad.shape[1]), xpad.dtype)
        xpad[halo + n_tok:, :] = jnp.zeros((halo, xpad.shape[1]), xpad.dtype)
    xpad[halo:halo + n_tok, :] = x_ref[...]
    for s in range(n_tok // slab):
        total = jnp.dot(win_ref[...], xpad[s * slab:s * slab + width, :], preferred_element_type=F32)
        tok = s * slab + lax.broadcasted_iota(jnp.int32, total.shape, 0)
        if grid_rows is None:
            count = jnp.minimum(tok + half, n_tok) - jnp.maximum(tok - half, 0)
        else:
            col = jnp.bitwise_and(tok, GRID_W - 1)
            row = jnp.right_shift(tok, GRID_W.bit_length() - 1)
            count = ((jnp.minimum(col + half, GRID_W) - jnp.maximum(col - half, 0))
                     * (jnp.minimum(row + half, grid_rows) - jnp.maximum(row - half, 0)))
        mean = total * (1.0 / count.astype(F32))
        d = (mean - x_ref[s * slab:(s + 1) * slab, :].astype(F32)).astype(BF16)
        y = jnp.dot(d, w_ref[...], preferred_element_type=F32) * sc_ref[...]
        o_ref[s * slab:(s + 1) * slab, :] = y.astype(o_ref.dtype)


def _pool_windows_2d():
    a = np.arange(ROW_TILE)
    s = np.arange(ROW_TILE + 2 * POOL_HALO)
    ra, ca = a // GRID_W, a % GRID_W
    rs, cs = s // GRID_W - POOL_HALO // GRID_W, s % GRID_W
    out = []
    for w in POOL_WINDOWS:
        h = w // 2
        dr = rs[None, :] - ra[:, None]
        dc = cs[None, :] - ca[:, None]
        out.append(((dr >= -h) & (dr <= h - 1) & (dc >= -h) & (dc <= h - 1)).astype(np.float32))
    return jnp.asarray(np.stack(out), BF16)


def _pool_windows_1d(n):
    a = np.arange(n)
    out = []
    for w in POOL_WINDOWS:
        h = w // 2
        ds = a[None, :] - a[:, None]
        out.append(((ds >= -h) & (ds <= h - 1)).astype(np.float32))
    return jnp.asarray(np.stack(out), BF16)


def _pool(p, pool_w, pool_scale, col0, n_batch, seq, ctx_len, with_ctx):
    n_groups, gw, _ = pool_w.shape
    bw = n_groups * gw
    assert POOL_HALO >= max(POOL_WINDOWS) // 2 * GRID_W and gw == ROW_TILE
    scale = pool_scale.reshape(1, bw)
    cb = col0 // gw

    def call(n_tok, halo, grid_rows, row0_blocks, windows):
        kern = functools.partial(_pool_kernel, n_tok=n_tok, halo=halo, grid_rows=grid_rows)
        return pl.pallas_call(
            kern,
            out_shape=jax.ShapeDtypeStruct((n_batch * n_tok, bw), BF16),
            grid=(n_batch, n_groups),
            in_specs=[
                pl.BlockSpec((n_tok, gw), lambda b, g: (row0_blocks + b, cb + g)),
                pl.BlockSpec((None,) + windows.shape[1:], lambda b, g: (g, 0, 0)),
                pl.BlockSpec((None, gw, gw), lambda b, g: (g, 0, 0)),
                pl.BlockSpec((1, gw), lambda b, g: (0, g)),
            ],
            out_specs=pl.BlockSpec((n_tok, gw), lambda b, g: (b, g)),
            scratch_shapes=[pltpu.VMEM((n_tok + 2 * halo, gw), BF16)],
            compiler_params=_params("arbitrary", "arbitrary"),
            name="pool_mix_ctx" if grid_rows is None else "pool_mix",
        )(p, windows, pool_w, scale)

    out = call(seq, POOL_HALO, seq // GRID_W, 0, _pool_windows_2d())
    if with_ctx:
        out_ctx = call(ctx_len, 0, None, n_batch * seq // ctx_len, _pool_windows_1d(ctx_len))
        out = jnp.concatenate([out, out_ctx], axis=0)
    return out


def _sgu_kernel(u_ref, v_ref, w_ref, b_ref, o_ref, *, n_groups, gw):
    v = v_ref[...].astype(F32)
    mu = jnp.mean(v, axis=-1, keepdims=True)
    vc = v - mu
    var = jnp.mean(vc * vc, axis=-1, keepdims=True)
    vn = (vc * lax.rsqrt(var + NORM_EPS)).astype(BF16)
    for kc in range(ROW_TILE // SGU_CHUNK):
        rs = slice(kc * SGU_CHUNK, (kc + 1) * SGU_CHUNK)
        for g in range(n_groups):
            cs = slice(g * gw, (g + 1) * gw)
            mixed = jnp.dot(w_ref[g], vn[rs, cs], preferred_element_type=F32) + b_ref[:, g:g + 1]
            o_ref[rs, cs] = (u_ref[rs, cs].astype(F32) * mixed).astype(o_ref.dtype)


def _sgu(p, sgu_w, sgu_b, col0, width, n_rows):
    n_groups = sgu_w.shape[0]
    gw = width // n_groups
    cb = col0 // width
    kern = functools.partial(_sgu_kernel, n_groups=n_groups, gw=gw)
    return pl.pallas_call(
        kern,
        out_shape=jax.ShapeDtypeStruct((n_rows, width), BF16),
        grid=(n_rows // ROW_TILE,),
        in_specs=[
            pl.BlockSpec((ROW_TILE, width), lambda i: (i, cb)),
            pl.BlockSpec((ROW_TILE, width), lambda i: (i, cb + 1)),
            pl.BlockSpec(sgu_w.shape, lambda i: (0, 0, 0)),
            pl.BlockSpec((SGU_CHUNK, n_groups), lambda i: (0, 0)),
        ],
        out_specs=pl.BlockSpec((ROW_TILE, width), lambda i: (i, 0)),
        compiler_params=_params("arbitrary"),
        name="spatial_gate",
    )(p, p, sgu_w.astype(BF16), sgu_b.T)


VMEM_ROW_PITCH = 24


def _rows_store(dst_ref, v, pitch):
    rows = v.shape[0]
    for s in range(v.shape[1] // LANES):
        dst_ref[pl.ds(s, rows, stride=pitch), :] = v[:, s * LANES:(s + 1) * LANES].astype(dst_ref.dtype)


def _rows_chunk(src_ref, s, rows, pitch):
    return src_ref[pl.ds(s, rows, stride=pitch), :]


def _router_kernel(x_ref, g_ref, sh_ref, sc_ref, wr_ref, br_ref, h_ref, eid_ref, gate_ref, *, n_exp):
    hf = _modulate(x_ref[...], g_ref[...], sh_ref[0], sc_ref[0])
    _rows_store(h_ref, hf, hf.shape[1] // LANES)
    h = hf.astype(BF16)
    logits = lax.dot_general(wr_ref[...], h, (((1,), (1,)), ((), ())), preferred_element_type=F32) + br_ref[...]
    e = jnp.exp(logits - jnp.max(logits, axis=0, keepdims=True))
    probs = e / jnp.sum(e, axis=0, keepdims=True)
    per = n_exp // N_EXPERT_GROUPS
    n_tok = probs.shape[1]
    lane_iota = lax.broadcasted_iota(jnp.int32, (per, n_tok), 0)
    best = None
    for grp in range(N_EXPERT_GROUPS):
        pg = probs[grp * per:(grp + 1) * per, :]
        m1 = jnp.max(pg, axis=0, keepdims=True)
        i1 = jnp.min(jnp.where(pg == m1, lane_iota, per), axis=0, keepdims=True)
        rest = jnp.where(lane_iota == i1, -1.0, pg)
        m2 = jnp.max(rest, axis=0, keepdims=True)
        i2 = jnp.min(jnp.where(rest == m2, lane_iota, per), axis=0, keepdims=True)
        cand = (m1 + m2, m1, m2, i1 + grp * per, i2 + grp * per)
        if best is None:
            best = cand
        else:
            take = cand[0] > best[0]
            best = tuple(jnp.where(take, cn, bs) for cn, bs in zip(cand, best))
    _, m1, m2, e1, e2 = best
    denom = m1 + m2
    eid_ref[0:1, :] = e1
    eid_ref[1:2, :] = e2
    gate_ref[0:1, :] = m1 / denom
    gate_ref[1:2, :] = m2 / denom


def _norm_router(x_tok, g, shift, scale, w_router, router_bias, n_rows, seq):
    d = x_tok.shape[1]
    n_exp = w_router.shape[1]
    n_batch = shift.shape[0] - 1
    mod = _mod_index(seq // ROW_TILE, n_batch)
    kern = functools.partial(_router_kernel, n_exp=n_exp)
    return pl.pallas_call(
        kern,
        out_shape=(
            jax.ShapeDtypeStruct((n_rows * (d // LANES), LANES), F32),
            jax.ShapeDtypeStruct((TOP_K, n_rows), jnp.int32),
            jax.ShapeDtypeStruct((TOP_K, n_rows), F32),
        ),
        grid=(n_rows // ROW_TILE,),
        in_specs=[
            pl.BlockSpec((ROW_TILE, d), lambda i: (i, 0)),
            pl.BlockSpec((1, d), lambda i: (0, 0)),
            pl.BlockSpec((1, 1, d), mod),
            pl.BlockSpec((1, 1, d), mod),
            pl.BlockSpec((n_exp, d), lambda i: (0, 0)),
            pl.BlockSpec((n_exp, 1), lambda i: (0, 0)),
        ],
        out_specs=(
            pl.BlockSpec((ROW_TILE * (d // LANES), LANES), lambda i: (i, 0)),
            pl.BlockSpec((TOP_K, ROW_TILE), lambda i: (0, i)),
            pl.BlockSpec((TOP_K, ROW_TILE), lambda i: (0, i)),
        ),
        compiler_params=_params("arbitrary"),
        name="norm_router",
    )(x_tok, g.reshape(1, d), shift, scale, w_router.T.astype(BF16), router_bias.reshape(n_exp, 1))


def _route_plan(eid, n_exp, n_chunks):
    n_tok = eid.shape[1]
    blk = ROW_TILE
    n_assign = TOP_K * n_tok
    flat_e = eid.reshape(-1)
    onehot = (flat_e[:, None] == jnp.arange(n_exp, dtype=jnp.int32)[None, :]).astype(jnp.int32)
    csum = jnp.cumsum(onehot, axis=0)
    counts = csum[-1]
    rank = jnp.sum(csum * onehot, axis=1) - 1
    padded = ((counts + blk - 1) // blk) * blk
    ends = jnp.cumsum(padded)
    start = ends - padded
    dest = (jnp.sum(start[None, :] * onehot, axis=1) + rank).astype(jnp.int32)
    n_blocks = -(-(n_assign + n_exp * (blk - 1)) // blk)
    row_assign = jnp.full((n_blocks * blk,), -1, jnp.int32).at[dest].set(jnp.arange(n_assign, dtype=jnp.int32))
    real = row_assign >= 0
    pos = jnp.arange(n_blocks * blk, dtype=jnp.int32) % blk
    row_src = jnp.where(real, row_assign % n_tok, 0) * n_chunks
    row_dst = jnp.where(real, row_assign, n_assign + pos) * n_chunks
    block_start = jnp.arange(n_blocks, dtype=jnp.int32) * blk
    block_e = jnp.minimum(jnp.sum((block_start[:, None] >= ends[None, :]).astype(jnp.int32), axis=1), n_exp - 1)
    n_used = (ends[-1] // blk).astype(jnp.int32).reshape(1)
    return (row_src.reshape(n_blocks, 1, blk), row_dst.reshape(n_blocks, 1, blk),
            block_e.astype(jnp.int32), n_used)


def _expert_kernel(be_ref, nu_ref, src0_ref, srcn_ref, dst_ref, tok_hbm, wgu_ref, wd_ref, out_hbm,
                   xbuf, xs, ystage, gsem, ssem, *, d_ff, n_chunks, n_assign):
    i = pl.program_id(0)
    nb = pl.num_programs(0)
    n_used = nu_ref[0]
    blk = xs.shape[0]
    pitch = VMEM_ROW_PITCH
    slot = lax.rem(i, 2)
    nslot = 1 - slot
    live = i < n_used
    block_rows = blk * n_chunks

    def gather_start(idx_ref, buf):
        for r in range(blk):
            src = pl.multiple_of(idx_ref[0, r], n_chunks)
            pltpu.make_async_copy(tok_hbm.at[pl.ds(src, n_chunks)],
                                  xbuf.at[buf, pl.ds(r * pitch, n_chunks)], gsem.at[buf]).start()

    def gather_wait(buf):
        pltpu.make_async_copy(tok_hbm.at[pl.ds(0, block_rows)],
                              xbuf.at[buf, pl.ds(0, block_rows)], gsem.at[buf]).wait()

    def scatter_wait(buf):
        pltpu.make_async_copy(ystage.at[buf, pl.ds(0, block_rows)],
                              out_hbm.at[pl.ds(0, block_rows)], ssem.at[buf]).wait()

    @pl.when(i == 0)
    def _prologue():
        ystage[0, pl.ds(0, block_rows), :] = jnp.zeros((block_rows, LANES), ystage.dtype)
        zero_fill = pltpu.make_async_copy(ystage.at[0, pl.ds(0, block_rows)],
                                          out_hbm.at[pl.ds(n_assign * n_chunks, block_rows)], ssem.at[0])
        zero_fill.start()
        zero_fill.wait()
        gather_start(src0_ref, 0)

    @pl.when(jnp.logical_and(live, i >= 2))
    def _reuse_stage():
        scatter_wait(slot)

    @pl.when(live)
    def _live():
        gather_wait(slot)
        gather_start(srcn_ref, nslot)
        for s in range(n_chunks):
            xs[:, s * LANES:(s + 1) * LANES] = _rows_chunk(xbuf.at[slot], s, blk, pitch).astype(xs.dtype)
        hgu = jnp.dot(xs[...], wgu_ref[...], preferred_element_type=F32)
        act = (_silu(hgu[:, :d_ff]) * hgu[:, d_ff:]).astype(BF16)
        _rows_store(ystage.at[slot], jnp.dot(act, wd_ref[...], preferred_element_type=F32), pitch)
        for r in range(blk):
            dst = pl.multiple_of(dst_ref[0, r], n_chunks)
            pltpu.make_async_copy(ystage.at[slot, pl.ds(r * pitch, n_chunks)],
                                  out_hbm.at[pl.ds(dst, n_chunks)], ssem.at[slot]).start()

    @pl.when(jnp.logical_and(live, i == n_used - 1))
    def _drain_scatter():
        scatter_wait(slot)

        @pl.when(i >= 1)
        def _previous():
            scatter_wait(nslot)

    @pl.when(i == n_used)
    def _drain_gather_idle():
        gather_wait(slot)

    @pl.when(jnp.logical_and(live, i == nb - 1))
    def _drain_gather_last():
        gather_wait(nslot)


def _experts(tokens, w_gu, w_down, layer, block_e, n_used, row_src, row_dst, n_assign):
    _, n_exp, d, two_ff = w_gu.shape
    d_ff = two_ff // 2
    blk = ROW_TILE
    n_chunks = d // LANES
    n_blocks = block_e.shape[0]
    kern = functools.partial(_expert_kernel, d_ff=d_ff, n_chunks=n_chunks, n_assign=n_assign)
    smem_row = lambda index: pl.BlockSpec((None, 1, blk), index, memory_space=pltpu.SMEM)
    return pl.pallas_call(
        kern,
        out_shape=jax.ShapeDtypeStruct(((n_assign + blk) * n_chunks, LANES), F32),
        grid_spec=pltpu.PrefetchScalarGridSpec(
            num_scalar_prefetch=2,
            grid=(n_blocks,),
            in_specs=[
                smem_row(lambda i, be, nu: (0, 0, 0)),
                smem_row(lambda i, be, nu: (jnp.minimum(i + 1, n_blocks - 1), 0, 0)),
                smem_row(lambda i, be, nu: (i, 0, 0)),
                pl.BlockSpec(memory_space=pl.ANY),
                pl.BlockSpec((None, None, d, two_ff), lambda i, be, nu: (layer, be[i], 0, 0)),
                pl.BlockSpec((None, None, d_ff, d), lambda i, be, nu: (layer, be[i], 0, 0)),
            ],
            out_specs=pl.BlockSpec(memory_space=pl.ANY),
            scratch_shapes=[
                pltpu.VMEM((2, blk * VMEM_ROW_PITCH, LANES), F32),
                pltpu.VMEM((blk, d), BF16),
                pltpu.VMEM((2, blk * VMEM_ROW_PITCH, LANES), F32),
                pltpu.SemaphoreType.DMA((2,)),
                pltpu.SemaphoreType.DMA((2,)),
            ],
        ),
        compiler_params=_params("arbitrary"),
        name="moe_experts",
    )(block_e, n_used, row_src, row_src, row_dst, tokens, w_gu, w_down)


def _combine_kernel(y0_ref, y1_ref, x_ref, gate_ref, g2_ref, fg_ref, o_ref, *, n_chunks, final_norm):
    blk, d = x_ref.shape
    gates = [jnp.broadcast_to(gate_ref[:, k:k + 1], (blk, LANES)) for k in range(TOP_K)]
    sq = jnp.zeros((blk, 1), F32)
    for s in range(n_chunks):
        cs = slice(s * LANES, (s + 1) * LANES)
        f = (_rows_chunk(y0_ref, s, blk, n_chunks) * gates[0]
             + _rows_chunk(y1_ref, s, blk, n_chunks) * gates[1])
        xo = x_ref[:, cs] + g2_ref[0, :, cs] * f
        o_ref[:, cs] = xo
        if final_norm:
            sq = sq + jnp.sum(xo * xo, axis=-1, keepdims=True)
    if final_norm:
        r = lax.rsqrt(sq * (1.0 / d) + NORM_EPS)
        for s in range(n_chunks):
            cs = slice(s * LANES, (s + 1) * LANES)
            o_ref[:, cs] = (o_ref[:, cs] * r) * fg_ref[:, cs]


def _combine(slots, x_tok, gates, g2, final_g, n_rows, seq, final_norm):
    assert TOP_K == 2
    d = x_tok.shape[1]
    n_batch = g2.shape[0] - 1
    blk = ROW_TILE
    n_chunks = d // LANES
    tiles = n_rows // blk
    mod = _mod_index(seq // blk, n_batch)
    kern = functools.partial(_combine_kernel, n_chunks=n_chunks, final_norm=final_norm)
    return pl.pallas_call(
        kern,
        out_shape=jax.ShapeDtypeStruct((n_rows, d), F32),
        grid=(tiles,),
        in_specs=[
            pl.BlockSpec((blk * n_chunks, LANES), lambda i: (i, 0)),
            pl.BlockSpec((blk * n_chunks, LANES), lambda i: (tiles + i, 0)),
            pl.BlockSpec((blk, d), lambda i: (i, 0)),
            pl.BlockSpec((blk, TOP_K), lambda i: (i, 0)),
            pl.BlockSpec((1, 1, d), mod),
            pl.BlockSpec((1, d), lambda i: (0, 0)),
        ],
        out_specs=pl.BlockSpec((blk, d), lambda i: (i, 0)),
        compiler_params=_params("arbitrary"),
        name="moe_combine",
    )(slots, slots, x_tok, gates.T, g2, final_g.reshape(1, d))


def kernel(x, c, ctx, c_ctx, w_ada, b_ada, norm1_g, norm2_g, w_in, ret_decay, pool_w, pool_scale, sgu_w, sgu_b,
           w_branch, w_out, w_router, router_bias, w_gu, w_down, final_g):
    n_batch, seq, d = x.shape
    ctx_len = ctx.shape[1]
    depth = w_ada.shape[0]
    bw = d // 2
    t_x = n_batch * seq
    t_all = t_x + n_batch * ctx_len
    n_exp = w_router.shape[1]
    assert seq % ROW_TILE == 0 and ctx_len % ROW_TILE == 0 and ctx_len % RET_CHUNK == 0
    assert GRID_W & (GRID_W - 1) == 0 and seq % GRID_W == 0

    x_tok = jnp.concatenate([x.reshape(t_x, d), ctx.reshape(n_batch * ctx_len, d)], axis=0)
    pad = (-(n_batch + 1)) % 8
    c_all = jnp.concatenate([c, c_ctx[None, :], jnp.zeros((pad, d), F32)], axis=0)
    ada = _ada(c_all, w_ada, b_ada)
    cos_t, sin_t = _rope_tables(seq, ctx_len)
    w_in_b, w_branch_b, w_out_b = w_in.astype(BF16), w_branch.astype(BF16), w_out.astype(BF16)
    w_gu_b, w_down_b = w_gu.astype(BF16), w_down.astype(BF16)

    out = None
    for l in range(depth):
        last = l == depth - 1
        mods = [m[:n_batch + 1, None, :] for m in jnp.split(ada[l], 6, axis=-1)]
        sh1, sc1, g1, sh2, sc2, g2 = mods
        rows_mix = t_x if last else t_all
        h = _norm(x_tok, norm1_g[l], sh1, sc1, t_all, seq)
        p = _matmul(h, w_in_b, l, t_all, tm=_tile(1024, t_all), tn=1024, name="in_proj")
        lg = jax.nn.log_sigmoid(ret_decay[l].astype(F32)).T
        lg = jnp.broadcast_to(lg[:, :, None], lg.shape + (RET_CHUNK,))
        ret = _retention(p, cos_t, sin_t, lg, n_batch, seq, ctx_len)
        pooled = _pool(p, pool_w[l].astype(BF16), pool_scale[l], 4 * bw, n_batch, seq, ctx_len, not last)
        gated = _sgu(p, sgu_w[l], sgu_b[l], 5 * bw, bw, rows_mix)
        merged = _merge((ret, pooled, gated), w_branch_b, l, p, 7 * bw, rows_mix)
        x_tok = _matmul_residual(merged, w_out_b, l, x_tok, g1, rows_mix, seq)
        h2, eid, gates = _norm_router(x_tok, norm2_g[l], sh2, sc2, w_router, router_bias, rows_mix, seq)
        row_src, row_dst, block_e, n_used = _route_plan(eid, n_exp, d // LANES)
        slots = _experts(h2, w_gu_b, w_down_b, l, block_e, n_used, row_src, row_dst, TOP_K * rows_mix)
        x_tok = _combine(slots, x_tok, gates, g2, final_g, rows_mix, seq, last)
        out = x_tok
    return out[:t_x].reshape(n_batch, seq, d)
```

```python
import functools

import numpy as np
import jax
import jax.numpy as jnp
from jax import lax
from jax.experimental import pallas as pl
from jax.experimental.pallas import tpu as pltpu

F32 = jnp.float32
BF16 = jnp.bfloat16

GRID_W = 64
RET_DK = 128
POOL_WINDOWS = (2, 4, 8, 16)
SGU_CHUNK = 128
N_EXPERT_GROUPS = 4
TOP_K = 2
ROPE_BASE = 10000.0
NORM_EPS = 1e-6

V7X_VMEM_LIMIT_BYTES = 60 * 1024 * 1024
LANES = 128

ROW_TILE = 256
RET_CHUNK = 256
POOL_HALO = 512


def _tile(pref, *sizes):
    t = pref
    while any(s % t for s in sizes):
        t //= 2
    return t


def _params(*sem):
    return pltpu.CompilerParams(dimension_semantics=sem, vmem_limit_bytes=V7X_VMEM_LIMIT_BYTES)


def _silu(x):
    return x * jax.nn.sigmoid(x)


def _ada_kernel(c_ref, w_ref, b_ref, o_ref):
    a = _silu(c_ref[...]).astype(BF16)
    o_ref[...] = jnp.dot(a, w_ref[...].astype(BF16), preferred_element_type=F32) + b_ref[...]


def _ada(c_all, w_ada, b_ada, tn=1024):
    depth, d, n = w_ada.shape
    rows = c_all.shape[0]
    return pl.pallas_call(
        _ada_kernel,
        out_shape=jax.ShapeDtypeStruct((depth, rows, n), F32),
        grid=(depth, n // tn),
        in_specs=[
            pl.BlockSpec((rows, d), lambda l, j: (0, 0)),
            pl.BlockSpec((None, d, tn), lambda l, j: (l, 0, j)),
            pl.BlockSpec((None, 1, tn), lambda l, j: (l, 0, j)),
        ],
        out_specs=pl.BlockSpec((None, rows, tn), lambda l, j: (l, 0, j)),
        compiler_params=_params("arbitrary", "arbitrary"),
        name="ada_proj",
    )(c_all, w_ada, b_ada.reshape(depth, 1, n))


def _modulate(x, g, shift, scale):
    r = lax.rsqrt(jnp.mean(x * x, axis=-1, keepdims=True) + NORM_EPS)
    return (x * r) * g * (1.0 + scale) + shift


def _norm_kernel(x_ref, g_ref, sh_ref, sc_ref, o_ref):
    o_ref[...] = _modulate(x_ref[...], g_ref[...], sh_ref[0], sc_ref[0]).astype(o_ref.dtype)


def _mod_index(tiles_per_batch, n_batch):
    return lambda i: (jnp.minimum(i // tiles_per_batch, n_batch), 0, 0)


def _norm(x_tok, g, shift, scale, n_rows, seq):
    t, d = x_tok.shape
    n_batch = shift.shape[0] - 1
    mod = _mod_index(seq // ROW_TILE, n_batch)
    return pl.pallas_call(
        _norm_kernel,
        out_shape=jax.ShapeDtypeStruct((n_rows, d), BF16),
        grid=(n_rows // ROW_TILE,),
        in_specs=[
            pl.BlockSpec((ROW_TILE, d), lambda i: (i, 0)),
            pl.BlockSpec((1, d), lambda i: (0, 0)),
            pl.BlockSpec((1, 1, d), mod),
            pl.BlockSpec((1, 1, d), mod),
        ],
        out_specs=pl.BlockSpec((ROW_TILE, d), lambda i: (i, 0)),
        compiler_params=_params("arbitrary"),
        name="mod_norm",
    )(x_tok, g.reshape(1, d), shift, scale)


def _mm_kernel(a_ref, b_ref, o_ref):
    o_ref[...] = jnp.dot(a_ref[...], b_ref[...], preferred_element_type=F32).astype(o_ref.dtype)


def _matmul(a, b, layer, n_rows, tm, tn, out_dtype=BF16, name="matmul"):
    k = a.shape[1]
    n = b.shape[2]
    return pl.pallas_call(
        _mm_kernel,
        out_shape=jax.ShapeDtypeStruct((n_rows, n), out_dtype),
        grid=(n_rows // tm, n // tn),
        in_specs=[
            pl.BlockSpec((tm, k), lambda i, j: (i, 0)),
            pl.BlockSpec((None, k, tn), lambda i, j: (layer, 0, j)),
        ],
        out_specs=pl.BlockSpec((tm, tn), lambda i, j: (i, j)),
        compiler_params=_params("arbitrary", "arbitrary"),
        name=name,
    )(a, b)


def _mm_castw_kernel(a_ref, b_ref, o_ref, wb):
    @pl.when(pl.program_id(1) == 0)
    def _cast():
        rows = wb.shape[0] // 8
        for c in range(8):
            wb[c * rows:(c + 1) * rows, :] = b_ref[c * rows:(c + 1) * rows, :].astype(wb.dtype)

    o_ref[...] = jnp.dot(a_ref[...], wb[...], preferred_element_type=F32).astype(o_ref.dtype)


def _matmul_castw(a, b, layer, n_rows, tm, tn, out_dtype=BF16, name="matmul_castw"):
    k = a.shape[1]
    n = b.shape[2]
    return pl.pallas_call(
        _mm_castw_kernel,
        out_shape=jax.ShapeDtypeStruct((n_rows, n), out_dtype),
        grid=(n // tn, n_rows // tm),
        in_specs=[
            pl.BlockSpec((tm, k), lambda j, i: (i, 0)),
            pl.BlockSpec((None, k, tn), lambda j, i: (layer, 0, j)),
        ],
        out_specs=pl.BlockSpec((tm, tn), lambda j, i: (i, j)),
        scratch_shapes=[pltpu.VMEM((k, tn), a.dtype)],
        compiler_params=_params("arbitrary", "arbitrary"),
        name=name,
    )(a, b)


def _mm_residual_kernel(a_ref, b_ref, x_ref, g_ref, o_ref):
    y = jnp.dot(a_ref[...], b_ref[...], preferred_element_type=F32)
    o_ref[...] = x_ref[...] + g_ref[0] * y


def _matmul_residual(a, b, layer, x_tok, gate, n_rows, seq, tn=1024):
    k = a.shape[1]
    n = b.shape[2]
    n_batch = gate.shape[0] - 1
    tm = _tile(1024, n_rows, seq)
    tiles_per_batch = seq // tm
    return pl.pallas_call(
        _mm_residual_kernel,
        out_shape=jax.ShapeDtypeStruct((n_rows, n), F32),
        grid=(n_rows // tm, n // tn),
        in_specs=[
            pl.BlockSpec((tm, k), lambda i, j: (i, 0)),
            pl.BlockSpec((None, k, tn), lambda i, j: (layer, 0, j)),
            pl.BlockSpec((tm, tn), lambda i, j: (i, j)),
            pl.BlockSpec((1, 1, tn), lambda i, j: (jnp.minimum(i // tiles_per_batch, n_batch), 0, j)),
        ],
        out_specs=pl.BlockSpec((tm, tn), lambda i, j: (i, j)),
        compiler_params=_params("arbitrary", "arbitrary"),
        name="out_proj_residual",
    )(a, b, x_tok, gate)


def _merge_kernel(a0_ref, a1_ref, a2_ref, b0_ref, b1_ref, b2_ref, g0_ref, g1_ref, g2_ref, o_ref):
    acc = None
    for a_ref, b_ref, g_ref in ((a0_ref, b0_ref, g0_ref), (a1_ref, b1_ref, g1_ref), (a2_ref, b2_ref, g2_ref)):
        y = jax.nn.sigmoid(g_ref[...].astype(F32)) * jnp.dot(a_ref[...], b_ref[...], preferred_element_type=F32)
        acc = y if acc is None else acc + y
    o_ref[...] = acc.astype(o_ref.dtype)


def _merge(branches, w_branch, layer, p, gate_col0, n_rows, tn=512):
    tm = _tile(1024, n_rows)
    kb = branches[0].shape[1]
    n = w_branch.shape[3]
    a_specs = [pl.BlockSpec((tm, kb), lambda i, j: (i, 0)) for _ in range(3)]
    b_specs = [pl.BlockSpec((None, None, kb, tn), functools.partial(lambda i, j, s: (layer, s, 0, j), s=s))
               for s in range(3)]
    g_specs = [
        pl.BlockSpec((tm, tn), functools.partial(lambda i, j, c: (i, c + j), c=(gate_col0 + s * n) // tn))
        for s in range(3)
    ]
    return pl.pallas_call(
        _merge_kernel,
        out_shape=jax.ShapeDtypeStruct((n_rows, n), BF16),
        grid=(n_rows // tm, n // tn),
        in_specs=a_specs + b_specs + g_specs,
        out_specs=pl.BlockSpec((tm, tn), lambda i, j: (i, j)),
        compiler_params=_params("arbitrary", "arbitrary"),
        name="branch_merge",
    )(*branches, w_branch, w_branch, w_branch, p, p, p)


def _retention_kernel(q_ref, k_ref, v_ref, g_ref, cos_ref, sin_ref, lg_ref, o_ref,
                      sf_all, s_run, dmat, rw, *, n_heads, nc_ctx, nc_x, qscale):
    pss = pl.program_id(1)
    j = pl.program_id(2)
    c = RET_CHUNK
    dk = RET_DK

    @pl.when(jnp.logical_and(pss == 0, j == 0))
    def _tables():
        row = lax.broadcasted_iota(jnp.int32, (c, c), 0)
        col = lax.broadcasted_iota(jnp.int32, (c, c), 1)
        diff = (row - col).astype(F32)
        pos = lax.broadcasted_iota(jnp.int32, (c, dk), 0).astype(F32)
        for h in range(n_heads):
            lgf = lg_ref[h, 0:1, :]
            lgb = lg_ref[h, 1:2, :]
            fwd = jnp.where(diff >= 0, jnp.exp(jnp.maximum(diff, 0.0) * lgf), 0.0)
            bwd = jnp.where(diff <= 0, jnp.exp(jnp.maximum(-diff, 0.0) * lgb), 0.0)
            dmat[h] = fwd + bwd
            lgf1 = lgf[:, :dk]
            lgb1 = lgb[:, :dk]
            rw[h, 0] = jnp.exp((pos + 1.0) * lgf1)
            rw[h, 1] = jnp.exp((c - pos) * lgb1)
            rw[h, 2] = jnp.exp((c - 1.0 - pos) * lgf1)
            rw[h, 3] = jnp.exp(pos * lgb1)

    @pl.when(j == 0)
    def _reset():
        s_run[...] = jnp.zeros_like(s_run)

    cosv = cos_ref[...]
    sinv = sin_ref[...]

    def rope(t):
        return t * cosv + pltpu.roll(t, dk // 2, 1) * sinv

    def state_update(h, kk, vv, direction):
        wv = (vv.astype(F32) * rw[h, 2 + direction]).astype(BF16)
        kv = lax.dot_general(kk, wv, (((0,), (0,)), ((), ())), preferred_element_type=F32)
        decay = jnp.exp(float(c) * lg_ref[h, direction:direction + 1, 0:dk])
        s_run[h] = decay * s_run[h] + kv

    @pl.when(pss == 0)
    def _forward():
        for h in range(n_heads):
            cs = slice(h * dk, (h + 1) * dk)
            sf_all[j, h] = s_run[h]
            kk = rope(k_ref[:, cs].astype(F32)).astype(BF16)
            state_update(h, kk, v_ref[:, cs], 0)

    @pl.when(pss == 1)
    def _backward():
        idx = jnp.where(j < nc_ctx, nc_ctx - 1 - j, nc_ctx + nc_x - 1 - (j - nc_ctx))
        for h in range(n_heads):
            cs = slice(h * dk, (h + 1) * dk)
            kk = rope(k_ref[:, cs].astype(F32)).astype(BF16)
            qq = (rope(q_ref[:, cs].astype(F32)) * qscale).astype(BF16)
            vv = v_ref[:, cs]
            scores = lax.dot_general(qq, kk, (((1,), (1,)), ((), ())), preferred_element_type=F32)
            inner = jnp.dot((scores * dmat[h]).astype(BF16), vv, preferred_element_type=F32)
            states = jnp.concatenate([sf_all[idx, h], s_run[h]], axis=1).astype(BF16)
            cross = jnp.dot(qq, states, preferred_element_type=F32)
            y = inner + rw[h, 0] * cross[:, :dk] + rw[h, 1] * cross[:, dk:]
            mu = jnp.mean(y, axis=-1, keepdims=True)
            yc = y - mu
            var = jnp.mean(yc * yc, axis=-1, keepdims=True)
            yn = yc * lax.rsqrt(var + NORM_EPS)
            o_ref[:, cs] = (yn * _silu(g_ref[:, cs].astype(F32))).astype(o_ref.dtype)
            state_update(h, kk, vv, 1)


def _retention(p, cos_t, sin_t, lg, n_batch, seq, ctx_len):
    bw = lg.shape[0] * RET_DK
    n_heads = lg.shape[0]
    c = RET_CHUNK
    nc_x = seq // c
    nc_ctx = ctx_len // c
    nc = nc_x + nc_ctx
    t = n_batch * (seq + ctx_len)

    def fwd_idx(pss, j):
        bwd = jnp.where(j < nc_ctx, nc_ctx - 1 - j, nc_ctx + nc_x - 1 - (j - nc_ctx))
        return jnp.where(pss == 0, j, bwd)

    def row_block(b, idx):
        return jnp.where(idx < nc_ctx, n_batch * nc_x + b * nc_ctx + idx, b * nc_x + (idx - nc_ctx))

    def kv_map(col):
        return lambda b, pss, j: (row_block(b, fwd_idx(pss, j)), col)

    def qg_map(col):
        return lambda b, pss, j: (row_block(b, fwd_idx(1, jnp.where(pss == 0, 0, j))), col)

    tab_map = lambda b, pss, j: (fwd_idx(pss, j), 0)
    kern = functools.partial(_retention_kernel, n_heads=n_heads, nc_ctx=nc_ctx, nc_x=nc_x,
                             qscale=float(RET_DK) ** -0.5)
    return pl.pallas_call(
        kern,
        out_shape=jax.ShapeDtypeStruct((t, bw), BF16),
        grid=(n_batch, 2, nc),
        in_specs=[
            pl.BlockSpec((c, bw), qg_map(0)),
            pl.BlockSpec((c, bw), kv_map(1)),
            pl.BlockSpec((c, bw), kv_map(2)),
            pl.BlockSpec((c, bw), qg_map(3)),
            pl.BlockSpec((c, RET_DK), tab_map),
            pl.BlockSpec((c, RET_DK), tab_map),
            pl.BlockSpec((n_heads, 2, c), lambda b, pss, j: (0, 0, 0)),
        ],
        out_specs=pl.BlockSpec((c, bw), qg_map(0)),
        scratch_shapes=[
            pltpu.VMEM((nc, n_heads, RET_DK, RET_DK), F32),
            pltpu.VMEM((n_heads, RET_DK, RET_DK), F32),
            pltpu.VMEM((n_heads, c, c), F32),
            pltpu.VMEM((n_heads, 4, c, RET_DK), F32),
        ],
        compiler_params=_params("arbitrary", "arbitrary", "arbitrary"),
        name="retention",
    )(p, p, p, p, cos_t, sin_t, lg)


def _rope_tables(seq, ctx_len):
    t = np.arange(seq)
    rows = (t // GRID_W).astype(np.float32)
    cols = (t % GRID_W).astype(np.float32)
    quarter = RET_DK // 4
    inv = jnp.asarray(ROPE_BASE, F32) ** (-jnp.arange(quarter, dtype=F32) / quarter)
    ang = jnp.concatenate([jnp.asarray(rows)[:, None] * inv, jnp.asarray(cols)[:, None] * inv], axis=-1)
    cos, sin = jnp.cos(ang), jnp.sin(ang)
    cos_t = jnp.concatenate([jnp.ones((ctx_len, RET_DK), F32), jnp.concatenate([cos, cos], axis=-1)], axis=0)
    sin_t = jnp.concatenate([jnp.zeros((ctx_len, RET_DK), F32), jnp.concatenate([-sin, sin], axis=-1)], axis=0)
    return cos_t, sin_t


def _pool_kernel(x_ref, win_ref, w_ref, sc_ref, o_ref, xpad, *, n_tok, halo, grid_rows):
    half = jnp.left_shift(1, pl.program_id(1))
    slab = ROW_TILE
    width = win_ref.shape[1]
    if halo:
        xpad[0:halo, :] = jnp.zeros((halo, xpad.shape[1]), xpad.dtype)
        xpad[halo + n_tok:, :] = jnp.zeros((halo, xpad.shape[1]), xpad.dtype)
    xpad[halo:halo + n_tok, :] = x_ref[...]
    for s in range(n_tok // slab):
        total = jnp.dot(win_ref[...], xpad[s * slab:s * slab + width, :], preferred_element_type=F32)
        tok = s * slab + lax.broadcasted_iota(jnp.int32, total.shape, 0)
        if grid_rows is None:
            count = jnp.minimum(tok + half, n_tok) - jnp.maximum(tok - half, 0)
        else:
            col = jnp.bitwise_and(tok, GRID_W - 1)
            row = jnp.right_shift(tok, GRID_W.bit_length() - 1)
            count = ((jnp.minimum(col + half, GRID_W) - jnp.maximum(col - half, 0))
                     * (jnp.minimum(row + half, grid_rows) - jnp.maximum(row - half, 0)))
        mean = total * (1.0 / count.astype(F32))
        d = (mean - x_ref[s * slab:(s + 1) * slab, :].astype(F32)).astype(BF16)
        y = jnp.dot(d, w_ref[...], preferred_element_type=F32) * sc_ref[...]
        o_ref[s * slab:(s + 1) * slab, :] = y.astype(o_ref.dtype)


def _pool_windows_2d():
    a = np.arange(ROW_TILE)
    s = np.arange(ROW_TILE + 2 * POOL_HALO)
    ra, ca = a // GRID_W, a % GRID_W
    rs, cs = s // GRID_W - POOL_HALO // GRID_W, s % GRID_W
    out = []
    for w in POOL_WINDOWS:
        h = w // 2
        dr = rs[None, :] - ra[:, None]
        dc = cs[None, :] - ca[:, None]
        out.append(((dr >= -h) & (dr <= h - 1) & (dc >= -h) & (dc <= h - 1)).astype(np.float32))
    return jnp.asarray(np.stack(out), BF16)


def _pool_windows_1d(n):
    a = np.arange(n)
    out = []
    for w in POOL_WINDOWS:
        h = w // 2
        ds = a[None, :] - a[:, None]
        out.append(((ds >= -h) & (ds <= h - 1)).astype(np.float32))
    return jnp.asarray(np.stack(out), BF16)


def _pool(p, pool_w, pool_scale, col0, n_batch, seq, ctx_len, with_ctx):
    n_groups, gw, _ = pool_w.shape
    bw = n_groups * gw
    assert POOL_HALO >= max(POOL_WINDOWS) // 2 * GRID_W and gw == ROW_TILE
    scale = pool_scale.reshape(1, bw)
    cb = col0 // gw

    def call(n_tok, halo, grid_rows, row0_blocks, windows):
        kern = functools.partial(_pool_kernel, n_tok=n_tok, halo=halo, grid_rows=grid_rows)
        return pl.pallas_call(
            kern,
            out_shape=jax.ShapeDtypeStruct((n_batch * n_tok, bw), BF16),
            grid=(n_batch, n_groups),
            in_specs=[
                pl.BlockSpec((n_tok, gw), lambda b, g: (row0_blocks + b, cb + g)),
                pl.BlockSpec((None,) + windows.shape[1:], lambda b, g: (g, 0, 0)),
                pl.BlockSpec((None, gw, gw), lambda b, g: (g, 0, 0)),
                pl.BlockSpec((1, gw), lambda b, g: (0, g)),
            ],
            out_specs=pl.BlockSpec((n_tok, gw), lambda b, g: (b, g)),
            scratch_shapes=[pltpu.VMEM((n_tok + 2 * halo, gw), BF16)],
            compiler_params=_params("arbitrary", "arbitrary"),
            name="pool_mix_ctx" if grid_rows is None else "pool_mix",
        )(p, windows, pool_w, scale)

    out = call(seq, POOL_HALO, seq // GRID_W, 0, _pool_windows_2d())
    if with_ctx:
        out_ctx = call(ctx_len, 0, None, n_batch * seq // ctx_len, _pool_windows_1d(ctx_len))
        out = jnp.concatenate([out, out_ctx], axis=0)
    return out


def _sgu_kernel(u_ref, v_ref, w_ref, b_ref, o_ref, *, n_groups, gw):
    v = v_ref[...].astype(F32)
    mu = jnp.mean(v, axis=-1, keepdims=True)
    vc = v - mu
    var = jnp.mean(vc * vc, axis=-1, keepdims=True)
    vn = (vc * lax.rsqrt(var + NORM_EPS)).astype(BF16)
    for kc in range(ROW_TILE // SGU_CHUNK):
        rs = slice(kc * SGU_CHUNK, (kc + 1) * SGU_CHUNK)
        for g in range(n_groups):
            cs = slice(g * gw, (g + 1) * gw)
            mixed = jnp.dot(w_ref[g], vn[rs, cs], preferred_element_type=F32) + b_ref[:, g:g + 1]
            o_ref[rs, cs] = (u_ref[rs, cs].astype(F32) * mixed).astype(o_ref.dtype)


def _sgu(p, sgu_w, sgu_b, col0, width, n_rows):
    n_groups = sgu_w.shape[0]
    gw = width // n_groups
    cb = col0 // width
    kern = functools.partial(_sgu_kernel, n_groups=n_groups, gw=gw)
    return pl.pallas_call(
        kern,
        out_shape=jax.ShapeDtypeStruct((n_rows, width), BF16),
        grid=(n_rows // ROW_TILE,),
        in_specs=[
            pl.BlockSpec((ROW_TILE, width), lambda i: (i, cb)),
            pl.BlockSpec((ROW_TILE, width), lambda i: (i, cb + 1)),
            pl.BlockSpec(sgu_w.shape, lambda i: (0, 0, 0)),
            pl.BlockSpec((SGU_CHUNK, n_groups), lambda i: (0, 0)),
        ],
        out_specs=pl.BlockSpec((ROW_TILE, width), lambda i: (i, 0)),
        compiler_params=_params("arbitrary"),
        name="spatial_gate",
    )(p, p, sgu_w.astype(BF16), sgu_b.T)


VMEM_ROW_PITCH = 24


def _rows_store(dst_ref, v, pitch):
    rows = v.shape[0]
    for s in range(v.shape[1] // LANES):
        dst_ref[pl.ds(s, rows, stride=pitch), :] = v[:, s * LANES:(s + 1) * LANES].astype(dst_ref.dtype)


def _zero_rows(ref, piece=512):
    for c in range(ref.shape[0] // piece):
        ref[c * piece:(c + 1) * piece, :] = jnp.zeros((piece, ref.shape[1]), ref.dtype)


def _rows_chunk(src_ref, s, rows, pitch):
    return src_ref[pl.ds(s, rows, stride=pitch), :]


def _router_kernel(x_ref, g_ref, sh_ref, sc_ref, wr_ref, br_ref, eid_ref, gate_ref, *, n_exp):
    h = _modulate(x_ref[...], g_ref[...], sh_ref[0], sc_ref[0]).astype(BF16)
    logits = lax.dot_general(wr_ref[...], h, (((1,), (1,)), ((), ())), preferred_element_type=F32) + br_ref[...]
    e = jnp.exp(logits - jnp.max(logits, axis=0, keepdims=True))
    probs = e / jnp.sum(e, axis=0, keepdims=True)
    per = n_exp // N_EXPERT_GROUPS
    n_tok = probs.shape[1]
    lane_iota = lax.broadcasted_iota(jnp.int32, (per, n_tok), 0)
    best = None
    for grp in range(N_EXPERT_GROUPS):
        pg = probs[grp * per:(grp + 1) * per, :]
        m1 = jnp.max(pg, axis=0, keepdims=True)
        i1 = jnp.min(jnp.where(pg == m1, lane_iota, per), axis=0, keepdims=True)
        rest = jnp.where(lane_iota == i1, -1.0, pg)
        m2 = jnp.max(rest, axis=0, keepdims=True)
        i2 = jnp.min(jnp.where(rest == m2, lane_iota, per), axis=0, keepdims=True)
        cand = (m1 + m2, m1, m2, i1 + grp * per, i2 + grp * per)
        if best is None:
            best = cand
        else:
            take = cand[0] > best[0]
            best = tuple(jnp.where(take, cn, bs) for cn, bs in zip(cand, best))
    _, m1, m2, e1, e2 = best
    denom = m1 + m2
    eid_ref[0:1, :] = e1
    eid_ref[1:2, :] = e2
    gate_ref[0:1, :] = m1 / denom
    gate_ref[1:2, :] = m2 / denom


def _norm_router(x_tok, g, shift, scale, w_router, router_bias, n_rows, seq):
    d = x_tok.shape[1]
    n_exp = w_router.shape[1]
    n_batch = shift.shape[0] - 1
    mod = _mod_index(seq // ROW_TILE, n_batch)
    kern = functools.partial(_router_kernel, n_exp=n_exp)
    return pl.pallas_call(
        kern,
        out_shape=(
            jax.ShapeDtypeStruct((TOP_K, n_rows), jnp.int32),
            jax.ShapeDtypeStruct((TOP_K, n_rows), F32),
        ),
        grid=(n_rows // ROW_TILE,),
        in_specs=[
            pl.BlockSpec((ROW_TILE, d), lambda i: (i, 0)),
            pl.BlockSpec((1, d), lambda i: (0, 0)),
            pl.BlockSpec((1, 1, d), mod),
            pl.BlockSpec((1, 1, d), mod),
            pl.BlockSpec((n_exp, d), lambda i: (0, 0)),
            pl.BlockSpec((n_exp, 1), lambda i: (0, 0)),
        ],
        out_specs=(
            pl.BlockSpec((TOP_K, ROW_TILE), lambda i: (0, i)),
            pl.BlockSpec((TOP_K, ROW_TILE), lambda i: (0, i)),
        ),
        compiler_params=_params("arbitrary"),
        name="norm_router",
    )(x_tok, g.reshape(1, d), shift, scale, w_router.T.astype(BF16), router_bias.reshape(n_exp, 1))


def _route_plan(eid, n_exp, n_chunks):
    n_tok = eid.shape[1]
    blk = ROW_TILE
    n_assign = TOP_K * n_tok
    tiles = n_tok // blk
    flat_e = eid.reshape(-1)
    onehot = flat_e[:, None] == jnp.arange(n_exp, dtype=jnp.int32)[None, :]
    oh = onehot.astype(BF16).reshape(n_assign // blk, blk, n_exp)
    tril = jnp.tril(jnp.ones((blk, blk), BF16))
    within = jnp.einsum("ij,cjk->cik", tril, oh, preferred_element_type=F32)
    totals = within[:, -1, :]
    offsets = jnp.cumsum(totals, axis=0) - totals
    csum = (within + offsets[:, None, :]).reshape(n_assign, n_exp).astype(jnp.int32)
    counts = csum[-1]
    rank = jnp.sum(jnp.where(onehot, csum, 0), axis=1) - 1
    padded = ((counts + blk - 1) // blk) * blk
    ends = jnp.cumsum(padded)
    start = ends - padded
    dest = (jnp.sum(jnp.where(onehot, start[None, :], 0), axis=1) + rank).astype(jnp.int32)
    dest_tiles = (dest * n_chunks).reshape(TOP_K, tiles, blk).transpose(1, 0, 2).reshape(tiles, 1, TOP_K * blk)
    n_blocks = -(-(n_assign + n_exp * (blk - 1)) // blk)
    block_start = jnp.arange(n_blocks, dtype=jnp.int32) * blk
    block_e = jnp.minimum(jnp.sum((block_start[:, None] >= ends[None, :]).astype(jnp.int32), axis=1), n_exp - 1)
    n_used = ends[-1] // blk
    live = jnp.arange(n_blocks) < n_used
    later = jnp.logical_and(live[None, :], block_e[None, :] > block_e[:, None])
    next_e = jnp.min(jnp.where(later, block_e[None, :], n_exp), axis=1)
    next_e = jnp.where(next_e == n_exp, block_e, next_e)
    i32 = lambda v: v.astype(jnp.int32)
    pad_start = (start + counts) * n_chunks
    pad_cnt = padded - counts
    return dest_tiles, i32(block_e), i32(n_used).reshape(1), i32(next_e), i32(pad_start), i32(pad_cnt)


def _dispatch_kernel(ps_ref, pc_ref, nu_ref, dst_ref, x_ref, g_ref, sh_ref, sc_ref, out_hbm,
                     stage, zbuf, sem, zsem, *, n_chunks):
    i = pl.program_id(0)
    n = pl.num_programs(0)
    blk = x_ref.shape[0]
    pitch = VMEM_ROW_PITCH
    slot = lax.rem(i, 2)
    tile_rows = blk * n_chunks
    n_blocks = out_hbm.shape[0] // tile_rows

    def scatter_wait(buf):
        for _ in range(TOP_K):
            pltpu.make_async_copy(stage.at[buf, pl.ds(0, tile_rows)],
                                  out_hbm.at[pl.ds(0, tile_rows)], sem.at[buf]).wait()

    @pl.when(i == 0)
    def _fill_padding():
        _zero_rows(zbuf)

        def pad_row(e, r):
            dst = pl.multiple_of(ps_ref[e] + r * n_chunks, n_chunks)
            return pltpu.make_async_copy(zbuf.at[pl.ds(0, n_chunks)], out_hbm.at[pl.ds(dst, n_chunks)], zsem)

        def idle_block(b):
            dst = pl.multiple_of(b * tile_rows, tile_rows)
            return pltpu.make_async_copy(zbuf, out_hbm.at[pl.ds(dst, tile_rows)], zsem)

        def for_all(op):
            def per_expert(e, carry):
                def per_row(r, c):
                    op(pad_row(e, r))
                    return c
                return lax.fori_loop(0, pc_ref[e], per_row, carry)
            lax.fori_loop(0, ps_ref.shape[0], per_expert, 0)

            def per_block(b, c):
                op(idle_block(b))
                return c
            lax.fori_loop(nu_ref[0], n_blocks, per_block, 0)

        for_all(lambda c: c.start())
        for_all(lambda c: c.wait())

    @pl.when(i >= 2)
    def _reuse_stage():
        scatter_wait(slot)

    _rows_store(stage.at[slot], _modulate(x_ref[...], g_ref[...], sh_ref[0], sc_ref[0]), pitch)
    for r in range(blk):
        for k in range(TOP_K):
            dst = pl.multiple_of(dst_ref[0, k * blk + r], n_chunks)
            pltpu.make_async_copy(stage.at[slot, pl.ds(r * pitch, n_chunks)],
                                  out_hbm.at[pl.ds(dst, n_chunks)], sem.at[slot]).start()

    @pl.when(i == n - 1)
    def _drain():
        scatter_wait(slot)

        @pl.when(i >= 1)
        def _previous():
            scatter_wait(1 - slot)


def _dispatch(x_tok, g, shift, scale, dest_tiles, pad_start, pad_cnt, n_used, n_rows, n_sorted, seq):
    d = x_tok.shape[1]
    n_chunks = d // LANES
    n_batch = shift.shape[0] - 1
    blk = ROW_TILE
    mod = _mod_index(seq // blk, n_batch)
    kern = functools.partial(_dispatch_kernel, n_chunks=n_chunks)
    return pl.pallas_call(
        kern,
        out_shape=jax.ShapeDtypeStruct((n_sorted * n_chunks, LANES), F32),
        grid_spec=pltpu.PrefetchScalarGridSpec(
            num_scalar_prefetch=3,
            grid=(n_rows // blk,),
            in_specs=[
                pl.BlockSpec((None, 1, TOP_K * blk), lambda i, *_: (i, 0, 0), memory_space=pltpu.SMEM),
                pl.BlockSpec((blk, d), lambda i, *_: (i, 0)),
                pl.BlockSpec((1, d), lambda i, *_: (0, 0)),
                pl.BlockSpec((1, 1, d), lambda i, *_: mod(i)),
                pl.BlockSpec((1, 1, d), lambda i, *_: mod(i)),
            ],
            out_specs=pl.BlockSpec(memory_space=pl.ANY),
            scratch_shapes=[
                pltpu.VMEM((2, blk * VMEM_ROW_PITCH, LANES), F32),
                pltpu.VMEM((blk * n_chunks, LANES), F32),
                pltpu.SemaphoreType.DMA((2,)),
                pltpu.SemaphoreType.DMA,
            ],
        ),
        compiler_params=_params("arbitrary"),
        name="moe_dispatch",
    )(pad_start, pad_cnt, n_used, dest_tiles, x_tok, g.reshape(1, d), shift, scale)


def _expert_kernel(be_ref, nu_ref, ne_ref, x_ref, wgu_hbm, wd_hbm, o_ref,
                   xs, act, wgu_b, wd_b, stage_gu, stage_d, wsem, *, layer, n_chunks):
    i = pl.program_id(0)
    n_used = nu_ref[0]
    blk, d = xs.shape
    d_ff = act.shape[1]
    live = i < n_used
    expert = be_ref[i]

    def weight_copies(e):
        return (pltpu.make_async_copy(wgu_hbm.at[layer, e], stage_gu, wsem.at[0]),
                pltpu.make_async_copy(wd_hbm.at[layer, e], stage_d, wsem.at[1]))

    @pl.when(i == 0)
    def _prologue():
        for c in weight_copies(expert):
            c.start()

    @pl.when(live)
    def _live():
        @pl.when(jnp.logical_or(i == 0, expert != be_ref[jnp.maximum(i - 1, 0)]))
        def _new_expert():
            for c in weight_copies(expert):
                c.wait()
            rows_gu = d // 8
            for c in range(8):
                rs = slice(c * rows_gu, (c + 1) * rows_gu)
                wgu_b[rs, :] = stage_gu[rs, :].astype(BF16)
            rows_d = d_ff // 4
            for c in range(4):
                rs = slice(c * rows_d, (c + 1) * rows_d)
                wd_b[rs, :] = stage_d[rs, :].astype(BF16)

            @pl.when(ne_ref[i] != expert)
            def _next_weights():
                for c in weight_copies(ne_ref[i]):
                    c.start()

        for s in range(n_chunks):
            xs[:, s * LANES:(s + 1) * LANES] = _rows_chunk(x_ref, s, blk, n_chunks).astype(xs.dtype)
        wide = 2 * LANES
        for c in range(d_ff // wide):
            a = jnp.dot(xs[...], wgu_b[:, c * wide:(c + 1) * wide], preferred_element_type=F32)
            gate = jnp.dot(xs[...], wgu_b[:, d_ff + c * wide:d_ff + (c + 1) * wide], preferred_element_type=F32)
            act[:, c * wide:(c + 1) * wide] = (_silu(a) * gate).astype(act.dtype)
        for c in range(d // (2 * wide)):
            y = jnp.dot(act[...], wd_b[:, c * 2 * wide:(c + 1) * 2 * wide], preferred_element_type=F32)
            for s in range(4):
                o_ref[pl.ds(4 * c + s, blk, stride=n_chunks), :] = y[:, s * LANES:(s + 1) * LANES]

    @pl.when(jnp.logical_not(live))
    def _idle():
        _zero_rows(o_ref)


def _experts(x_sorted, w_gu, w_down, layer, block_e, n_used, next_e):
    _, n_exp, d, two_ff = w_gu.shape
    d_ff = two_ff // 2
    blk = ROW_TILE
    n_chunks = d // LANES
    n_blocks = block_e.shape[0]
    kern = functools.partial(_expert_kernel, layer=layer, n_chunks=n_chunks)
    return pl.pallas_call(
        kern,
        out_shape=jax.ShapeDtypeStruct((n_blocks * blk * n_chunks, LANES), F32),
        grid_spec=pltpu.PrefetchScalarGridSpec(
            num_scalar_prefetch=3,
            grid=(n_blocks,),
            in_specs=[
                pl.BlockSpec((blk * n_chunks, LANES), lambda i, *_: (i, 0)),
                pl.BlockSpec(memory_space=pl.ANY),
                pl.BlockSpec(memory_space=pl.ANY),
            ],
            out_specs=pl.BlockSpec((blk * n_chunks, LANES), lambda i, *_: (i, 0)),
            scratch_shapes=[
                pltpu.VMEM((blk, d), BF16),
                pltpu.VMEM((blk, d_ff), BF16),
                pltpu.VMEM((d, two_ff), BF16),
                pltpu.VMEM((d_ff, d), BF16),
                pltpu.VMEM((d, two_ff), F32),
                pltpu.VMEM((d_ff, d), F32),
                pltpu.SemaphoreType.DMA((2,)),
            ],
        ),
        compiler_params=_params("arbitrary"),
        name="moe_experts",
    )(block_e, n_used, next_e, x_sorted, w_gu, w_down)


def _combine_kernel(dst0_ref, dstn_ref, ys_hbm, x_ref, gate_ref, g2_ref, fg_ref, o_ref, ybuf, sem,
                    *, n_chunks, final_norm):
    i = pl.program_id(0)
    n = pl.num_programs(0)
    blk, d = x_ref.shape
    pitch = VMEM_ROW_PITCH
    slot = lax.rem(i, 2)
    tile_rows = blk * n_chunks

    def gather_start(idx_ref, buf):
        for r in range(blk):
            for k in range(TOP_K):
                src = pl.multiple_of(idx_ref[0, k * blk + r], n_chunks)
                pltpu.make_async_copy(ys_hbm.at[pl.ds(src, n_chunks)],
                                      ybuf.at[buf, k, pl.ds(r * pitch, n_chunks)], sem.at[buf]).start()

    @pl.when(i == 0)
    def _prologue():
        gather_start(dst0_ref, 0)

    @pl.when(i + 1 < n)
    def _prefetch():
        gather_start(dstn_ref, 1 - slot)

    for k in range(TOP_K):
        pltpu.make_async_copy(ys_hbm.at[pl.ds(0, tile_rows)],
                              ybuf.at[slot, k, pl.ds(0, tile_rows)], sem.at[slot]).wait()
    gates = [jnp.broadcast_to(gate_ref[:, k:k + 1], (blk, LANES)) for k in range(TOP_K)]
    sq = jnp.zeros((blk, 1), F32)
    for s in range(n_chunks):
        cs = slice(s * LANES, (s + 1) * LANES)
        f = (_rows_chunk(ybuf.at[slot, 0], s, blk, pitch) * gates[0]
             + _rows_chunk(ybuf.at[slot, 1], s, blk, pitch) * gates[1])
        xo = x_ref[:, cs] + g2_ref[0, :, cs] * f
        o_ref[:, cs] = xo
        if final_norm:
            sq = sq + jnp.sum(xo * xo, axis=-1, keepdims=True)
    if final_norm:
        r = lax.rsqrt(sq * (1.0 / d) + NORM_EPS)
        for s in range(n_chunks):
            cs = slice(s * LANES, (s + 1) * LANES)
            o_ref[:, cs] = (o_ref[:, cs] * r) * fg_ref[:, cs]


def _combine(ys, dest_tiles, x_tok, gates, g2, final_g, n_rows, seq, final_norm):
    assert TOP_K == 2
    d = x_tok.shape[1]
    n_batch = g2.shape[0] - 1
    blk = ROW_TILE
    n_chunks = d // LANES
    tiles = n_rows // blk
    mod = _mod_index(seq // blk, n_batch)
    kern = functools.partial(_combine_kernel, n_chunks=n_chunks, final_norm=final_norm)
    smem_row = lambda index: pl.BlockSpec((None, 1, TOP_K * blk), index, memory_space=pltpu.SMEM)
    return pl.pallas_call(
        kern,
        out_shape=jax.ShapeDtypeStruct((n_rows, d), F32),
        grid=(tiles,),
        in_specs=[
            smem_row(lambda i: (0, 0, 0)),
            smem_row(lambda i: (jnp.minimum(i + 1, tiles - 1), 0, 0)),
            pl.BlockSpec(memory_space=pl.ANY),
            pl.BlockSpec((blk, d), lambda i: (i, 0)),
            pl.BlockSpec((blk, TOP_K), lambda i: (i, 0)),
            pl.BlockSpec((1, 1, d), mod),
            pl.BlockSpec((1, d), lambda i: (0, 0)),
        ],
        out_specs=pl.BlockSpec((blk, d), lambda i: (i, 0)),
        scratch_shapes=[
            pltpu.VMEM((2, TOP_K, blk * VMEM_ROW_PITCH, LANES), F32),
            pltpu.SemaphoreType.DMA((2,)),
        ],
        compiler_params=_params("arbitrary"),
        name="moe_combine",
    )(dest_tiles, dest_tiles, ys, x_tok, gates.T, g2, final_g.reshape(1, d))


def kernel(x, c, ctx, c_ctx, w_ada, b_ada, norm1_g, norm2_g, w_in, ret_decay, pool_w, pool_scale, sgu_w, sgu_b,
           w_branch, w_out, w_router, router_bias, w_gu, w_down, final_g):
    n_batch, seq, d = x.shape
    ctx_len = ctx.shape[1]
    depth = w_ada.shape[0]
    bw = d // 2
    t_x = n_batch * seq
    t_all = t_x + n_batch * ctx_len
    n_exp = w_router.shape[1]
    assert seq % ROW_TILE == 0 and ctx_len % ROW_TILE == 0 and ctx_len % RET_CHUNK == 0
    assert GRID_W & (GRID_W - 1) == 0 and seq % GRID_W == 0

    x_tok = jnp.concatenate([x.reshape(t_x, d), ctx.reshape(n_batch * ctx_len, d)], axis=0)
    pad = (-(n_batch + 1)) % 8
    c_all = jnp.concatenate([c, c_ctx[None, :], jnp.zeros((pad, d), F32)], axis=0)
    ada = _ada(c_all, w_ada, b_ada)
    cos_t, sin_t = _rope_tables(seq, ctx_len)
    w_branch_b, w_out_b = w_branch.astype(BF16), w_out.astype(BF16)

    out = None
    for l in range(depth):
        last = l == depth - 1
        mods = [m[:n_batch + 1, None, :] for m in jnp.split(ada[l], 6, axis=-1)]
        sh1, sc1, g1, sh2, sc2, g2 = mods
        rows_mix = t_x if last else t_all
        h = _norm(x_tok, norm1_g[l], sh1, sc1, t_all, seq)
        p = _matmul_castw(h, w_in, l, t_all, tm=_tile(1024, t_all), tn=1024, name="in_proj")
        lg = jax.nn.log_sigmoid(ret_decay[l].astype(F32)).T
        lg = jnp.broadcast_to(lg[:, :, None], lg.shape + (RET_CHUNK,))
        ret = _retention(p, cos_t, sin_t, lg, n_batch, seq, ctx_len)
        pooled = _pool(p, pool_w[l].astype(BF16), pool_scale[l], 4 * bw, n_batch, seq, ctx_len, not last)
        gated = _sgu(p, sgu_w[l], sgu_b[l], 5 * bw, bw, rows_mix)
        merged = _merge((ret, pooled, gated), w_branch_b, l, p, 7 * bw, rows_mix)
        x_tok = _matmul_residual(merged, w_out_b, l, x_tok, g1, rows_mix, seq)
        eid, gates = _norm_router(x_tok, norm2_g[l], sh2, sc2, w_router, router_bias, rows_mix, seq)
        dest_tiles, block_e, n_used, next_e, pad_start, pad_cnt = _route_plan(eid, n_exp, d // LANES)
        x_sorted = _dispatch(x_tok, norm2_g[l], sh2, sc2, dest_tiles, pad_start, pad_cnt, n_used,
                             rows_mix, block_e.shape[0] * ROW_TILE, seq)
        ys = _experts(x_sorted, w_gu, w_down, l, block_e, n_used, next_e)
        x_tok = _combine(ys, dest_tiles, x_tok, gates, g2, final_g, rows_mix, seq, last)
        out = x_tok
    return out[:t_x].reshape(n_batch, seq, d)
```

```python
import functools

import numpy as np
import jax
import jax.numpy as jnp
from jax import lax
from jax.experimental import pallas as pl
from jax.experimental.pallas import tpu as pltpu

F32 = jnp.float32
BF16 = jnp.bfloat16

GRID_W = 64
RET_DK = 128
POOL_WINDOWS = (2, 4, 8, 16)
SGU_CHUNK = 128
N_EXPERT_GROUPS = 4
TOP_K = 2
ROPE_BASE = 10000.0
NORM_EPS = 1e-6

V7X_VMEM_LIMIT_BYTES = 60 * 1024 * 1024
LANES = 128

ROW_TILE = 256
RET_CHUNK = 256
POOL_HALO = 512


def _tile(pref, *sizes):
    t = pref
    while any(s % t for s in sizes):
        t //= 2
    return t


def _lane_tile(cap, size):
    return max(t for t in range(LANES, cap + 1, LANES) if size % t == 0)


def _params(*sem):
    return pltpu.CompilerParams(dimension_semantics=sem, vmem_limit_bytes=V7X_VMEM_LIMIT_BYTES)


def _silu(x):
    return x * jax.nn.sigmoid(x)


def _ada_kernel(c_ref, w_ref, b_ref, o_ref):
    a = _silu(c_ref[...]).astype(BF16)
    o_ref[...] = jnp.dot(a, w_ref[...].astype(BF16), preferred_element_type=F32) + b_ref[...]


def _ada(c_all, w_ada, b_ada, tn=1024):
    depth, d, n = w_ada.shape
    rows = c_all.shape[0]
    return pl.pallas_call(
        _ada_kernel,
        out_shape=jax.ShapeDtypeStruct((depth, rows, n), F32),
        grid=(depth, n // tn),
        in_specs=[
            pl.BlockSpec((rows, d), lambda l, j: (0, 0)),
            pl.BlockSpec((None, d, tn), lambda l, j: (l, 0, j)),
            pl.BlockSpec((None, 1, tn), lambda l, j: (l, 0, j)),
        ],
        out_specs=pl.BlockSpec((None, rows, tn), lambda l, j: (l, 0, j)),
        compiler_params=_params("arbitrary", "arbitrary"),
        name="ada_proj",
    )(c_all, w_ada, b_ada.reshape(depth, 1, n))


def _modulate(x, g, shift, scale):
    r = lax.rsqrt(jnp.mean(x * x, axis=-1, keepdims=True) + NORM_EPS)
    return (x * r) * g * (1.0 + scale) + shift


def _two_source_specs(block, main_tiles, grid_rank):
    main = lambda i: jnp.minimum(i, main_tiles - 1)
    ctx = lambda i: jnp.maximum(i - main_tiles, 0)
    if grid_rank == 1:
        return pl.BlockSpec(block, lambda i: (main(i), 0)), pl.BlockSpec(block, lambda i: (ctx(i), 0))
    return pl.BlockSpec(block, lambda i, j: (main(i), j)), pl.BlockSpec(block, lambda i, j: (ctx(i), j))


def _two_source_pick(i, main_tiles, main_ref, ctx_ref):
    return jnp.where(i < main_tiles, main_ref[...], ctx_ref[...])


def _norm_kernel(xm_ref, xc_ref, g_ref, sh_ref, sc_ref, o_ref, *, main_tiles):
    x = _two_source_pick(pl.program_id(0), main_tiles, xm_ref, xc_ref)
    o_ref[...] = _modulate(x, g_ref[...], sh_ref[0], sc_ref[0]).astype(o_ref.dtype)


def _mod_index(tiles_per_batch, n_batch):
    return lambda i: (jnp.minimum(i // tiles_per_batch, n_batch), 0, 0)


def _norm(x_main, x_ctx, g, shift, scale, seq):
    d = x_main.shape[1]
    n_rows = x_main.shape[0] + x_ctx.shape[0]
    n_batch = shift.shape[0] - 1
    main_tiles = x_main.shape[0] // ROW_TILE
    mod = _mod_index(seq // ROW_TILE, n_batch)
    xm_spec, xc_spec = _two_source_specs((ROW_TILE, d), main_tiles, 1)
    return pl.pallas_call(
        functools.partial(_norm_kernel, main_tiles=main_tiles),
        out_shape=jax.ShapeDtypeStruct((n_rows, d), BF16),
        grid=(n_rows // ROW_TILE,),
        in_specs=[
            xm_spec,
            xc_spec,
            pl.BlockSpec((1, d), lambda i: (0, 0)),
            pl.BlockSpec((1, 1, d), mod),
            pl.BlockSpec((1, 1, d), mod),
        ],
        out_specs=pl.BlockSpec((ROW_TILE, d), lambda i: (i, 0)),
        compiler_params=_params("arbitrary"),
        name="mod_norm",
    )(x_main, x_ctx, g.reshape(1, d), shift, scale)


def _mm_kernel(a_ref, b_ref, o_ref):
    o_ref[...] = jnp.dot(a_ref[...], b_ref[...], preferred_element_type=F32).astype(o_ref.dtype)


def _matmul(a, b, layer, n_rows, tm, tn, out_dtype=BF16, name="matmul"):
    k = a.shape[1]
    n = b.shape[2]
    return pl.pallas_call(
        _mm_kernel,
        out_shape=jax.ShapeDtypeStruct((n_rows, n), out_dtype),
        grid=(n_rows // tm, n // tn),
        in_specs=[
            pl.BlockSpec((tm, k), lambda i, j: (i, 0)),
            pl.BlockSpec((None, k, tn), lambda i, j: (layer, 0, j)),
        ],
        out_specs=pl.BlockSpec((tm, tn), lambda i, j: (i, j)),
        compiler_params=_params("arbitrary", "arbitrary"),
        name=name,
    )(a, b)


def _mm_castw_kernel(a_ref, b_ref, o_ref, wb):
    @pl.when(pl.program_id(1) == 0)
    def _cast():
        rows = wb.shape[0] // 8
        for c in range(8):
            wb[c * rows:(c + 1) * rows, :] = b_ref[c * rows:(c + 1) * rows, :].astype(wb.dtype)

    o_ref[...] = jnp.dot(a_ref[...], wb[...], preferred_element_type=F32).astype(o_ref.dtype)


def _matmul_castw(a, b, layer, n_rows, tm, tn, out_dtype=BF16, name="matmul_castw"):
    k = a.shape[1]
    n = b.shape[2]
    return pl.pallas_call(
        _mm_castw_kernel,
        out_shape=jax.ShapeDtypeStruct((n_rows, n), out_dtype),
        grid=(n // tn, n_rows // tm),
        in_specs=[
            pl.BlockSpec((tm, k), lambda j, i: (i, 0)),
            pl.BlockSpec((None, k, tn), lambda j, i: (layer, 0, j)),
        ],
        out_specs=pl.BlockSpec((tm, tn), lambda j, i: (i, j)),
        scratch_shapes=[pltpu.VMEM((k, tn), a.dtype)],
        compiler_params=_params("arbitrary", "arbitrary"),
        name=name,
    )(a, b)


def _mm_residual_kernel(a_ref, b_ref, xm_ref, xc_ref, g_ref, o_ref, *, main_tiles):
    y = jnp.dot(a_ref[...], b_ref[...], preferred_element_type=F32)
    o_ref[...] = _two_source_pick(pl.program_id(0), main_tiles, xm_ref, xc_ref) + g_ref[0] * y


def _matmul_residual(a, b, layer, x_main, x_ctx, gate, n_rows, seq, tn=1024):
    k = a.shape[1]
    n = b.shape[2]
    n_batch = gate.shape[0] - 1
    tm = _tile(1024, n_rows, seq, x_ctx.shape[0])
    tiles_per_batch = seq // tm
    main_tiles = n_batch * tiles_per_batch
    xm_spec, xc_spec = _two_source_specs((tm, tn), main_tiles, 2)
    return pl.pallas_call(
        functools.partial(_mm_residual_kernel, main_tiles=main_tiles),
        out_shape=jax.ShapeDtypeStruct((n_rows, n), F32),
        grid=(n_rows // tm, n // tn),
        in_specs=[
            pl.BlockSpec((tm, k), lambda i, j: (i, 0)),
            pl.BlockSpec((None, k, tn), lambda i, j: (layer, 0, j)),
            xm_spec,
            xc_spec,
            pl.BlockSpec((1, 1, tn), lambda i, j: (jnp.minimum(i // tiles_per_batch, n_batch), 0, j)),
        ],
        out_specs=pl.BlockSpec((tm, tn), lambda i, j: (i, j)),
        compiler_params=_params("arbitrary", "arbitrary"),
        name="out_proj_residual",
    )(a, b, x_main, x_ctx, gate)


def _merge_kernel(a0_ref, a1_ref, a2_ref, b0_ref, b1_ref, b2_ref, g0_ref, g1_ref, g2_ref, o_ref):
    acc = None
    for a_ref, b_ref, g_ref in ((a0_ref, b0_ref, g0_ref), (a1_ref, b1_ref, g1_ref), (a2_ref, b2_ref, g2_ref)):
        y = jax.nn.sigmoid(g_ref[...].astype(F32)) * jnp.dot(a_ref[...], b_ref[...], preferred_element_type=F32)
        acc = y if acc is None else acc + y
    o_ref[...] = acc.astype(o_ref.dtype)


def _merge(branches, w_branch, layer, p, gate_col0, n_rows, tn=1024):
    tm = _tile(1024, n_rows)
    kb = branches[0].shape[1]
    n = w_branch.shape[3]
    a_specs = [pl.BlockSpec((tm, kb), lambda i, j: (i, 0)) for _ in range(3)]
    b_specs = [pl.BlockSpec((None, None, kb, tn), functools.partial(lambda i, j, s: (layer, s, 0, j), s=s))
               for s in range(3)]
    g_specs = [
        pl.BlockSpec((tm, tn), functools.partial(lambda i, j, c: (i, c + j), c=(gate_col0 + s * n) // tn))
        for s in range(3)
    ]
    return pl.pallas_call(
        _merge_kernel,
        out_shape=jax.ShapeDtypeStruct((n_rows, n), BF16),
        grid=(n_rows // tm, n // tn),
        in_specs=a_specs + b_specs + g_specs,
        out_specs=pl.BlockSpec((tm, tn), lambda i, j: (i, j)),
        compiler_params=_params("arbitrary", "arbitrary"),
        name="branch_merge",
    )(*branches, w_branch, w_branch, w_branch, p, p, p)


def _retention_kernel(q_ref, k_ref, v_ref, g_ref, cos_ref, sin_ref, lg_ref, o_ref,
                      sf_all, s_run, dmat, rw, *, n_heads, nc_ctx, nc_x, qscale):
    pss = pl.program_id(1)
    j = pl.program_id(2)
    c = RET_CHUNK
    dk = RET_DK

    @pl.when(jnp.logical_and(pss == 0, j == 0))
    def _tables():
        row = lax.broadcasted_iota(jnp.int32, (c, c), 0)
        col = lax.broadcasted_iota(jnp.int32, (c, c), 1)
        diff = (row - col).astype(F32)
        pos = lax.broadcasted_iota(jnp.int32, (c, dk), 0).astype(F32)
        for h in range(n_heads):
            lgf = lg_ref[h, 0:1, :]
            lgb = lg_ref[h, 1:2, :]
            fwd = jnp.where(diff >= 0, jnp.exp(jnp.maximum(diff, 0.0) * lgf), 0.0)
            bwd = jnp.where(diff <= 0, jnp.exp(jnp.maximum(-diff, 0.0) * lgb), 0.0)
            dmat[h] = fwd + bwd
            lgf1 = lgf[:, :dk]
            lgb1 = lgb[:, :dk]
            rw[h, 0] = jnp.exp((pos + 1.0) * lgf1)
            rw[h, 1] = jnp.exp((c - pos) * lgb1)
            rw[h, 2] = jnp.exp((c - 1.0 - pos) * lgf1)
            rw[h, 3] = jnp.exp(pos * lgb1)

    @pl.when(j == 0)
    def _reset():
        s_run[...] = jnp.zeros_like(s_run)

    cosv = cos_ref[...]
    sinv = sin_ref[...]

    def rope(t):
        return t * cosv + pltpu.roll(t, dk // 2, 1) * sinv

    def state_update(h, kk, vv, direction):
        wv = (vv.astype(F32) * rw[h, 2 + direction]).astype(BF16)
        kv = lax.dot_general(kk, wv, (((0,), (0,)), ((), ())), preferred_element_type=F32)
        decay = jnp.exp(float(c) * lg_ref[h, direction:direction + 1, 0:dk])
        s_run[h] = decay * s_run[h] + kv

    @pl.when(pss == 0)
    def _forward():
        for h in range(n_heads):
            cs = slice(h * dk, (h + 1) * dk)
            sf_all[j, h] = s_run[h]
            kk = rope(k_ref[:, cs].astype(F32)).astype(BF16)
            state_update(h, kk, v_ref[:, cs], 0)

    @pl.when(pss == 1)
    def _backward():
        idx = jnp.where(j < nc_ctx, nc_ctx - 1 - j, nc_ctx + nc_x - 1 - (j - nc_ctx))
        for h in range(n_heads):
            cs = slice(h * dk, (h + 1) * dk)
            kk = rope(k_ref[:, cs].astype(F32)).astype(BF16)
            qq = (rope(q_ref[:, cs].astype(F32)) * qscale).astype(BF16)
            vv = v_ref[:, cs]
            scores = lax.dot_general(qq, kk, (((1,), (1,)), ((), ())), preferred_element_type=F32)
            inner = jnp.dot((scores * dmat[h]).astype(BF16), vv, preferred_element_type=F32)
            states = jnp.concatenate([sf_all[idx, h], s_run[h]], axis=1).astype(BF16)
            cross = jnp.dot(qq, states, preferred_element_type=F32)
            y = inner + rw[h, 0] * cross[:, :dk] + rw[h, 1] * cross[:, dk:]
            mu = jnp.mean(y, axis=-1, keepdims=True)
            yc = y - mu
            var = jnp.mean(yc * yc, axis=-1, keepdims=True)
            yn = yc * lax.rsqrt(var + NORM_EPS)
            o_ref[:, cs] = (yn * _silu(g_ref[:, cs].astype(F32))).astype(o_ref.dtype)
            state_update(h, kk, vv, 1)


def _retention(p, cos_t, sin_t, lg, n_batch, seq, ctx_len):
    bw = lg.shape[0] * RET_DK
    n_heads = lg.shape[0]
    c = RET_CHUNK
    nc_x = seq // c
    nc_ctx = ctx_len // c
    nc = nc_x + nc_ctx
    t = n_batch * (seq + ctx_len)

    def fwd_idx(pss, j):
        bwd = jnp.where(j < nc_ctx, nc_ctx - 1 - j, nc_ctx + nc_x - 1 - (j - nc_ctx))
        return jnp.where(pss == 0, j, bwd)

    def row_block(b, idx):
        return jnp.where(idx < nc_ctx, n_batch * nc_x + b * nc_ctx + idx, b * nc_x + (idx - nc_ctx))

    def kv_map(col):
        return lambda b, pss, j: (row_block(b, fwd_idx(pss, j)), col)

    def qg_map(col):
        return lambda b, pss, j: (row_block(b, fwd_idx(1, jnp.where(pss == 0, 0, j))), col)

    tab_map = lambda b, pss, j: (fwd_idx(pss, j), 0)
    kern = functools.partial(_retention_kernel, n_heads=n_heads, nc_ctx=nc_ctx, nc_x=nc_x,
                             qscale=float(RET_DK) ** -0.5)
    return pl.pallas_call(
        kern,
        out_shape=jax.ShapeDtypeStruct((t, bw), BF16),
        grid=(n_batch, 2, nc),
        in_specs=[
            pl.BlockSpec((c, bw), qg_map(0)),
            pl.BlockSpec((c, bw), kv_map(1)),
            pl.BlockSpec((c, bw), kv_map(2)),
            pl.BlockSpec((c, bw), qg_map(3)),
            pl.BlockSpec((c, RET_DK), tab_map),
            pl.BlockSpec((c, RET_DK), tab_map),
            pl.BlockSpec((n_heads, 2, c), lambda b, pss, j: (0, 0, 0)),
        ],
        out_specs=pl.BlockSpec((c, bw), qg_map(0)),
        scratch_shapes=[
            pltpu.VMEM((nc, n_heads, RET_DK, RET_DK), F32),
            pltpu.VMEM((n_heads, RET_DK, RET_DK), F32),
            pltpu.VMEM((n_heads, c, c), F32),
            pltpu.VMEM((n_heads, 4, c, RET_DK), F32),
        ],
        compiler_params=_params("arbitrary", "arbitrary", "arbitrary"),
        name="retention",
    )(p, p, p, p, cos_t, sin_t, lg)


def _rope_tables(seq, ctx_len):
    t = np.arange(seq)
    rows = (t // GRID_W).astype(np.float32)
    cols = (t % GRID_W).astype(np.float32)
    quarter = RET_DK // 4
    inv = jnp.asarray(ROPE_BASE, F32) ** (-jnp.arange(quarter, dtype=F32) / quarter)
    ang = jnp.concatenate([jnp.asarray(rows)[:, None] * inv, jnp.asarray(cols)[:, None] * inv], axis=-1)
    cos, sin = jnp.cos(ang), jnp.sin(ang)
    cos_t = jnp.concatenate([jnp.ones((ctx_len, RET_DK), F32), jnp.concatenate([cos, cos], axis=-1)], axis=0)
    sin_t = jnp.concatenate([jnp.zeros((ctx_len, RET_DK), F32), jnp.concatenate([-sin, sin], axis=-1)], axis=0)
    return cos_t, sin_t


def _pool_kernel(x_ref, win_ref, w_ref, sc_ref, o_ref, xpad, *, n_tok, halo, grid_rows):
    half = jnp.left_shift(1, pl.program_id(1))
    slab = ROW_TILE
    width = win_ref.shape[1]
    if halo:
        xpad[0:halo, :] = jnp.zeros((halo, xpad.shape[1]), xpad.dtype)
        xpad[halo + n_tok:, :] = jnp.zeros((halo, xpad.shape[1]), xpad.dtype)
    xpad[halo:halo + n_tok, :] = x_ref[...]
    for s in range(n_tok // slab):
        total = jnp.dot(win_ref[...], xpad[s * slab:s * slab + width, :], preferred_element_type=F32)
        tok = s * slab + lax.broadcasted_iota(jnp.int32, total.shape, 0)
        if grid_rows is None:
            count = jnp.minimum(tok + half, n_tok) - jnp.maximum(tok - half, 0)
        else:
            col = jnp.bitwise_and(tok, GRID_W - 1)
            row = jnp.right_shift(tok, GRID_W.bit_length() - 1)
            count = ((jnp.minimum(col + half, GRID_W) - jnp.maximum(col - half, 0))
                     * (jnp.minimum(row + half, grid_rows) - jnp.maximum(row - half, 0)))
        mean = total * (1.0 / count.astype(F32))
        d = (mean - x_ref[s * slab:(s + 1) * slab, :].astype(F32)).astype(BF16)
        y = jnp.dot(d, w_ref[...], preferred_element_type=F32) * sc_ref[...]
        o_ref[s * slab:(s + 1) * slab, :] = y.astype(o_ref.dtype)


def _pool_windows_2d():
    a = np.arange(ROW_TILE)
    s = np.arange(ROW_TILE + 2 * POOL_HALO)
    ra, ca = a // GRID_W, a % GRID_W
    rs, cs = s // GRID_W - POOL_HALO // GRID_W, s % GRID_W
    out = []
    for w in POOL_WINDOWS:
        h = w // 2
        dr = rs[None, :] - ra[:, None]
        dc = cs[None, :] - ca[:, None]
        out.append(((dr >= -h) & (dr <= h - 1) & (dc >= -h) & (dc <= h - 1)).astype(np.float32))
    return jnp.asarray(np.stack(out), BF16)


def _pool_windows_1d(n):
    a = np.arange(n)
    out = []
    for w in POOL_WINDOWS:
        h = w // 2
        ds = a[None, :] - a[:, None]
        out.append(((ds >= -h) & (ds <= h - 1)).astype(np.float32))
    return jnp.asarray(np.stack(out), BF16)


def _pool(p, pool_w, pool_scale, col0, n_batch, seq, ctx_len, with_ctx):
    n_groups, gw, _ = pool_w.shape
    bw = n_groups * gw
    assert POOL_HALO >= max(POOL_WINDOWS) // 2 * GRID_W and gw == ROW_TILE
    scale = pool_scale.reshape(1, bw)
    cb = col0 // gw

    def call(n_tok, halo, grid_rows, row0_blocks, windows):
        kern = functools.partial(_pool_kernel, n_tok=n_tok, halo=halo, grid_rows=grid_rows)
        return pl.pallas_call(
            kern,
            out_shape=jax.ShapeDtypeStruct((n_batch * n_tok, bw), BF16),
            grid=(n_batch, n_groups),
            in_specs=[
                pl.BlockSpec((n_tok, gw), lambda b, g: (row0_blocks + b, cb + g)),
                pl.BlockSpec((None,) + windows.shape[1:], lambda b, g: (g, 0, 0)),
                pl.BlockSpec((None, gw, gw), lambda b, g: (g, 0, 0)),
                pl.BlockSpec((1, gw), lambda b, g: (0, g)),
            ],
            out_specs=pl.BlockSpec((n_tok, gw), lambda b, g: (b, g)),
            scratch_shapes=[pltpu.VMEM((n_tok + 2 * halo, gw), BF16)],
            compiler_params=_params("arbitrary", "arbitrary"),
            name="pool_mix_ctx" if grid_rows is None else "pool_mix",
        )(p, windows, pool_w, scale)

    out = call(seq, POOL_HALO, seq // GRID_W, 0, _pool_windows_2d())
    if with_ctx:
        out_ctx = call(ctx_len, 0, None, n_batch * seq // ctx_len, _pool_windows_1d(ctx_len))
        out = jnp.concatenate([out, out_ctx], axis=0)
    return out


def _sgu_kernel(u_ref, v_ref, w_ref, b_ref, o_ref, *, n_groups, gw):
    v = v_ref[...].astype(F32)
    mu = jnp.mean(v, axis=-1, keepdims=True)
    vc = v - mu
    var = jnp.mean(vc * vc, axis=-1, keepdims=True)
    vn = (vc * lax.rsqrt(var + NORM_EPS)).astype(BF16)
    for kc in range(ROW_TILE // SGU_CHUNK):
        rs = slice(kc * SGU_CHUNK, (kc + 1) * SGU_CHUNK)
        for g in range(n_groups):
            cs = slice(g * gw, (g + 1) * gw)
            mixed = jnp.dot(w_ref[g], vn[rs, cs], preferred_element_type=F32) + b_ref[:, g:g + 1]
            o_ref[rs, cs] = (u_ref[rs, cs].astype(F32) * mixed).astype(o_ref.dtype)


def _sgu(p, sgu_w, sgu_b, col0, width, n_rows):
    n_groups = sgu_w.shape[0]
    gw = width // n_groups
    cb = col0 // width
    kern = functools.partial(_sgu_kernel, n_groups=n_groups, gw=gw)
    return pl.pallas_call(
        kern,
        out_shape=jax.ShapeDtypeStruct((n_rows, width), BF16),
        grid=(n_rows // ROW_TILE,),
        in_specs=[
            pl.BlockSpec((ROW_TILE, width), lambda i: (i, cb)),
            pl.BlockSpec((ROW_TILE, width), lambda i: (i, cb + 1)),
            pl.BlockSpec(sgu_w.shape, lambda i: (0, 0, 0)),
            pl.BlockSpec((SGU_CHUNK, n_groups), lambda i: (0, 0)),
        ],
        out_specs=pl.BlockSpec((ROW_TILE, width), lambda i: (i, 0)),
        compiler_params=_params("arbitrary"),
        name="spatial_gate",
    )(p, p, sgu_w.astype(BF16), sgu_b.T)


VMEM_ROW_PITCH = 24
DMA_PRIORITIES = 2


def _rows_store(dst_ref, v, pitch):
    rows = v.shape[0]
    for s in range(v.shape[1] // LANES):
        dst_ref[pl.ds(s, rows, stride=pitch), :] = v[:, s * LANES:(s + 1) * LANES].astype(dst_ref.dtype)


def _zero_rows(ref, piece=512):
    for c in range(ref.shape[0] // piece):
        ref[c * piece:(c + 1) * piece, :] = jnp.zeros((piece, ref.shape[1]), ref.dtype)


def _rows_chunk(src_ref, s, rows, pitch):
    return src_ref[pl.ds(s, rows, stride=pitch), :]


def _router_kernel(x_ref, g_ref, sh_ref, sc_ref, wr_ref, br_ref, eid_ref, gate_ref, *, n_exp):
    h = _modulate(x_ref[...], g_ref[...], sh_ref[0], sc_ref[0]).astype(BF16)
    logits = lax.dot_general(wr_ref[...], h, (((1,), (1,)), ((), ())), preferred_element_type=F32) + br_ref[...]
    e = jnp.exp(logits - jnp.max(logits, axis=0, keepdims=True))
    probs = e / jnp.sum(e, axis=0, keepdims=True)
    per = n_exp // N_EXPERT_GROUPS
    n_tok = probs.shape[1]
    lane_iota = lax.broadcasted_iota(jnp.int32, (per, n_tok), 0)
    best = None
    for grp in range(N_EXPERT_GROUPS):
        pg = probs[grp * per:(grp + 1) * per, :]
        m1 = jnp.max(pg, axis=0, keepdims=True)
        i1 = jnp.min(jnp.where(pg == m1, lane_iota, per), axis=0, keepdims=True)
        rest = jnp.where(lane_iota == i1, -1.0, pg)
        m2 = jnp.max(rest, axis=0, keepdims=True)
        i2 = jnp.min(jnp.where(rest == m2, lane_iota, per), axis=0, keepdims=True)
        cand = (m1 + m2, m1, m2, i1 + grp * per, i2 + grp * per)
        if best is None:
            best = cand
        else:
            take = cand[0] > best[0]
            best = tuple(jnp.where(take, cn, bs) for cn, bs in zip(cand, best))
    _, m1, m2, e1, e2 = best
    denom = m1 + m2
    eid_ref[0:1, :] = e1
    eid_ref[1:2, :] = e2
    gate_ref[0:1, :] = m1 / denom
    gate_ref[1:2, :] = m2 / denom


def _norm_router(x_tok, g, shift, scale, w_router, router_bias, n_rows, seq):
    d = x_tok.shape[1]
    n_exp = w_router.shape[1]
    n_batch = shift.shape[0] - 1
    mod = _mod_index(seq // ROW_TILE, n_batch)
    kern = functools.partial(_router_kernel, n_exp=n_exp)
    return pl.pallas_call(
        kern,
        out_shape=(
            jax.ShapeDtypeStruct((TOP_K, n_rows), jnp.int32),
            jax.ShapeDtypeStruct((TOP_K, n_rows), F32),
        ),
        grid=(n_rows // ROW_TILE,),
        in_specs=[
            pl.BlockSpec((ROW_TILE, d), lambda i: (i, 0)),
            pl.BlockSpec((1, d), lambda i: (0, 0)),
            pl.BlockSpec((1, 1, d), mod),
            pl.BlockSpec((1, 1, d), mod),
            pl.BlockSpec((n_exp, d), lambda i: (0, 0)),
            pl.BlockSpec((n_exp, 1), lambda i: (0, 0)),
        ],
        out_specs=(
            pl.BlockSpec((TOP_K, ROW_TILE), lambda i: (0, i)),
            pl.BlockSpec((TOP_K, ROW_TILE), lambda i: (0, i)),
        ),
        compiler_params=_params("arbitrary"),
        name="norm_router",
    )(x_tok, g.reshape(1, d), shift, scale, w_router.T.astype(BF16), router_bias.reshape(n_exp, 1))


def _route_plan(eid, n_exp, n_chunks):
    n_tok = eid.shape[1]
    blk = ROW_TILE
    n_assign = TOP_K * n_tok
    tiles = n_tok // blk
    flat_e = eid.reshape(-1)
    onehot = flat_e[:, None] == jnp.arange(n_exp, dtype=jnp.int32)[None, :]
    oh = onehot.astype(BF16).reshape(n_assign // blk, blk, n_exp)
    tril = jnp.tril(jnp.ones((blk, blk), BF16))
    within = jnp.einsum("ij,cjk->cik", tril, oh, preferred_element_type=F32)
    totals = within[:, -1, :]
    offsets = jnp.cumsum(totals, axis=0) - totals
    csum = (within + offsets[:, None, :]).reshape(n_assign, n_exp).astype(jnp.int32)
    counts = csum[-1]
    rank = jnp.sum(jnp.where(onehot, csum, 0), axis=1) - 1
    padded = ((counts + blk - 1) // blk) * blk
    ends = jnp.cumsum(padded)
    start = ends - padded
    dest = (jnp.sum(jnp.where(onehot, start[None, :], 0), axis=1) + rank).astype(jnp.int32)
    dest_tiles = (dest * n_chunks).reshape(TOP_K, tiles, blk).transpose(1, 0, 2).reshape(tiles, 1, TOP_K * blk)
    n_blocks = -(-(n_assign + n_exp * (blk - 1)) // blk)
    block_start = jnp.arange(n_blocks, dtype=jnp.int32) * blk
    block_e = jnp.minimum(jnp.sum((block_start[:, None] >= ends[None, :]).astype(jnp.int32), axis=1), n_exp - 1)
    n_used = ends[-1] // blk
    live = jnp.arange(n_blocks) < n_used
    later = jnp.logical_and(live[None, :], block_e[None, :] > block_e[:, None])
    next_e = jnp.min(jnp.where(later, block_e[None, :], n_exp), axis=1)
    next_e = jnp.where(next_e == n_exp, block_e, next_e)
    i32 = lambda v: v.astype(jnp.int32)
    pad_start = (start + counts) * n_chunks
    pad_cnt = padded - counts
    return dest_tiles, i32(block_e), i32(n_used).reshape(1), i32(next_e), i32(pad_start), i32(pad_cnt)


def _dispatch_kernel(ps_ref, pc_ref, nu_ref, dst_ref, x_ref, g_ref, sh_ref, sc_ref, out_hbm,
                     stage, zbuf, sem, zsem, *, n_chunks):
    i = pl.program_id(0)
    n = pl.num_programs(0)
    blk = x_ref.shape[0]
    pitch = VMEM_ROW_PITCH
    slot = lax.rem(i, 2)
    tile_rows = blk * n_chunks
    n_blocks = out_hbm.shape[0] // tile_rows

    def scatter_wait(buf):
        for _ in range(TOP_K):
            pltpu.make_async_copy(stage.at[buf, pl.ds(0, tile_rows)],
                                  out_hbm.at[pl.ds(0, tile_rows)], sem.at[buf]).wait()

    @pl.when(i == 0)
    def _fill_padding():
        _zero_rows(zbuf)

        def pad_row(e, r):
            dst = pl.multiple_of(ps_ref[e] + r * n_chunks, n_chunks)
            return pltpu.make_async_copy(zbuf.at[pl.ds(0, n_chunks)], out_hbm.at[pl.ds(dst, n_chunks)], zsem)

        def idle_block(b):
            dst = pl.multiple_of(b * tile_rows, tile_rows)
            return pltpu.make_async_copy(zbuf, out_hbm.at[pl.ds(dst, tile_rows)], zsem)

        def for_all(op):
            def per_expert(e, carry):
                def per_row(r, c):
                    op(pad_row(e, r))
                    return c
                return lax.fori_loop(0, pc_ref[e], per_row, carry)
            lax.fori_loop(0, ps_ref.shape[0], per_expert, 0)

            def per_block(b, c):
                op(idle_block(b))
                return c
            lax.fori_loop(nu_ref[0], n_blocks, per_block, 0)

        for_all(lambda c: c.start())
        for_all(lambda c: c.wait())

    @pl.when(i >= 2)
    def _reuse_stage():
        scatter_wait(slot)

    _rows_store(stage.at[slot], _modulate(x_ref[...], g_ref[...], sh_ref[0], sc_ref[0]), pitch)
    for r in range(blk):
        for k in range(TOP_K):
            dst = pl.multiple_of(dst_ref[0, k * blk + r], n_chunks)
            pltpu.make_async_copy(stage.at[slot, pl.ds(r * pitch, n_chunks)],
                                  out_hbm.at[pl.ds(dst, n_chunks)], sem.at[slot]).start(priority=k % DMA_PRIORITIES)

    @pl.when(i == n - 1)
    def _drain():
        scatter_wait(slot)

        @pl.when(i >= 1)
        def _previous():
            scatter_wait(1 - slot)


def _dispatch(x_tok, g, shift, scale, dest_tiles, pad_start, pad_cnt, n_used, n_rows, n_sorted, seq):
    d = x_tok.shape[1]
    n_chunks = d // LANES
    n_batch = shift.shape[0] - 1
    blk = ROW_TILE
    mod = _mod_index(seq // blk, n_batch)
    kern = functools.partial(_dispatch_kernel, n_chunks=n_chunks)
    return pl.pallas_call(
        kern,
        out_shape=jax.ShapeDtypeStruct((n_sorted * n_chunks, LANES), F32),
        grid_spec=pltpu.PrefetchScalarGridSpec(
            num_scalar_prefetch=3,
            grid=(n_rows // blk,),
            in_specs=[
                pl.BlockSpec((None, 1, TOP_K * blk), lambda i, *_: (i, 0, 0), memory_space=pltpu.SMEM),
                pl.BlockSpec((blk, d), lambda i, *_: (i, 0)),
                pl.BlockSpec((1, d), lambda i, *_: (0, 0)),
                pl.BlockSpec((1, 1, d), lambda i, *_: mod(i)),
                pl.BlockSpec((1, 1, d), lambda i, *_: mod(i)),
            ],
            out_specs=pl.BlockSpec(memory_space=pl.ANY),
            scratch_shapes=[
                pltpu.VMEM((2, blk * VMEM_ROW_PITCH, LANES), F32),
                pltpu.VMEM((blk * n_chunks, LANES), F32),
                pltpu.SemaphoreType.DMA((2,)),
                pltpu.SemaphoreType.DMA,
            ],
        ),
        compiler_params=_params("arbitrary"),
        name="moe_dispatch",
    )(pad_start, pad_cnt, n_used, dest_tiles, x_tok, g.reshape(1, d), shift, scale)


def _expert_kernel(be_ref, nu_ref, ne_ref, x_ref, wgu_hbm, wd_hbm, o_ref,
                   xs, act, wgu_b, wd_b, stage_gu, stage_d, wsem, *, layer, n_chunks):
    i = pl.program_id(0)
    n_used = nu_ref[0]
    blk, d = xs.shape
    d_ff = act.shape[1]
    live = i < n_used
    expert = be_ref[i]

    def weight_copies(e):
        return (pltpu.make_async_copy(wgu_hbm.at[layer, e], stage_gu, wsem.at[0]),
                pltpu.make_async_copy(wd_hbm.at[layer, e], stage_d, wsem.at[1]))

    @pl.when(i == 0)
    def _prologue():
        for c in weight_copies(expert):
            c.start()

    @pl.when(live)
    def _live():
        @pl.when(jnp.logical_or(i == 0, expert != be_ref[jnp.maximum(i - 1, 0)]))
        def _new_expert():
            for c in weight_copies(expert):
                c.wait()
            rows_gu = d // 8
            for c in range(8):
                rs = slice(c * rows_gu, (c + 1) * rows_gu)
                wgu_b[rs, :] = stage_gu[rs, :].astype(BF16)
            rows_d = d_ff // 4
            for c in range(4):
                rs = slice(c * rows_d, (c + 1) * rows_d)
                wd_b[rs, :] = stage_d[rs, :].astype(BF16)

            @pl.when(ne_ref[i] != expert)
            def _next_weights():
                for c in weight_copies(ne_ref[i]):
                    c.start()

        for s in range(n_chunks):
            xs[:, s * LANES:(s + 1) * LANES] = _rows_chunk(x_ref, s, blk, n_chunks).astype(xs.dtype)
        wide = 2 * LANES
        for c in range(d_ff // wide):
            a = jnp.dot(xs[...], wgu_b[:, c * wide:(c + 1) * wide], preferred_element_type=F32)
            gate = jnp.dot(xs[...], wgu_b[:, d_ff + c * wide:d_ff + (c + 1) * wide], preferred_element_type=F32)
            act[:, c * wide:(c + 1) * wide] = (_silu(a) * gate).astype(act.dtype)
        for c in range(d // (2 * wide)):
            y = jnp.dot(act[...], wd_b[:, c * 2 * wide:(c + 1) * 2 * wide], preferred_element_type=F32)
            for s in range(4):
                o_ref[pl.ds(4 * c + s, blk, stride=n_chunks), :] = y[:, s * LANES:(s + 1) * LANES]

    @pl.when(jnp.logical_not(live))
    def _idle():
        _zero_rows(o_ref)


def _experts(x_sorted, w_gu, w_down, layer, block_e, n_used, next_e):
    _, n_exp, d, two_ff = w_gu.shape
    d_ff = two_ff // 2
    blk = ROW_TILE
    n_chunks = d // LANES
    n_blocks = block_e.shape[0]
    kern = functools.partial(_expert_kernel, layer=layer, n_chunks=n_chunks)
    return pl.pallas_call(
        kern,
        out_shape=jax.ShapeDtypeStruct((n_blocks * blk * n_chunks, LANES), F32),
        grid_spec=pltpu.PrefetchScalarGridSpec(
            num_scalar_prefetch=3,
            grid=(n_blocks,),
            in_specs=[
                pl.BlockSpec((blk * n_chunks, LANES), lambda i, *_: (i, 0)),
                pl.BlockSpec(memory_space=pl.ANY),
                pl.BlockSpec(memory_space=pl.ANY),
            ],
            out_specs=pl.BlockSpec((blk * n_chunks, LANES), lambda i, *_: (i, 0)),
            scratch_shapes=[
                pltpu.VMEM((blk, d), BF16),
                pltpu.VMEM((blk, d_ff), BF16),
                pltpu.VMEM((d, two_ff), BF16),
                pltpu.VMEM((d_ff, d), BF16),
                pltpu.VMEM((d, two_ff), F32),
                pltpu.VMEM((d_ff, d), F32),
                pltpu.SemaphoreType.DMA((2,)),
            ],
        ),
        compiler_params=_params("arbitrary"),
        name="moe_experts",
    )(block_e, n_used, next_e, x_sorted, w_gu, w_down)


def _combine_kernel(dst0_ref, dstn_ref, ys_hbm, x_ref, gate_ref, g2_ref, ng_ref, nsh_ref, nsc_ref, *refs,
                    n_chunks, final_norm):
    if final_norm:
        o_ref, ybuf, sem = refs
        h_ref = o_ref
    else:
        o_ref, h_ref, ybuf, sem = refs
    i = pl.program_id(0)
    n = pl.num_programs(0)
    blk, d = x_ref.shape
    pitch = VMEM_ROW_PITCH
    slot = lax.rem(i, 2)
    tile_rows = blk * n_chunks

    def gather_start(idx_ref, buf):
        for r in range(blk):
            for k in range(TOP_K):
                src = pl.multiple_of(idx_ref[0, k * blk + r], n_chunks)
                pltpu.make_async_copy(ys_hbm.at[pl.ds(src, n_chunks)],
                                      ybuf.at[buf, k, pl.ds(r * pitch, n_chunks)],
                                      sem.at[buf]).start(priority=k % DMA_PRIORITIES)

    @pl.when(i == 0)
    def _prologue():
        gather_start(dst0_ref, 0)

    @pl.when(i + 1 < n)
    def _prefetch():
        gather_start(dstn_ref, 1 - slot)

    for k in range(TOP_K):
        pltpu.make_async_copy(ys_hbm.at[pl.ds(0, tile_rows)],
                              ybuf.at[slot, k, pl.ds(0, tile_rows)], sem.at[slot]).wait()
    gates = [jnp.broadcast_to(gate_ref[:, k:k + 1], (blk, LANES)) for k in range(TOP_K)]
    sq = jnp.zeros((blk, 1), F32)
    for s in range(n_chunks):
        cs = slice(s * LANES, (s + 1) * LANES)
        f = (_rows_chunk(ybuf.at[slot, 0], s, blk, pitch) * gates[0]
             + _rows_chunk(ybuf.at[slot, 1], s, blk, pitch) * gates[1])
        xo = x_ref[:, cs] + g2_ref[0, :, cs] * f
        o_ref[:, cs] = xo
        sq = sq + jnp.sum(xo * xo, axis=-1, keepdims=True)
    r = lax.rsqrt(sq * (1.0 / d) + NORM_EPS)
    for s in range(n_chunks):
        cs = slice(s * LANES, (s + 1) * LANES)
        hn = (o_ref[:, cs] * r) * ng_ref[:, cs] * (1.0 + nsc_ref[0, :, cs]) + nsh_ref[0, :, cs]
        h_ref[:, cs] = hn.astype(h_ref.dtype)


def _combine(ys, dest_tiles, x_tok, gates, g2, norm_g, norm_shift, norm_scale, n_rows, seq, final_norm):
    assert TOP_K == 2
    d = x_tok.shape[1]
    n_batch = g2.shape[0] - 1
    blk = ROW_TILE
    n_chunks = d // LANES
    tiles = n_rows // blk
    mod = _mod_index(seq // blk, n_batch)
    kern = functools.partial(_combine_kernel, n_chunks=n_chunks, final_norm=final_norm)
    smem_row = lambda index: pl.BlockSpec((None, 1, TOP_K * blk), index, memory_space=pltpu.SMEM)
    row_spec = pl.BlockSpec((blk, d), lambda i: (i, 0))
    if final_norm:
        out_shape, out_specs = jax.ShapeDtypeStruct((n_rows, d), F32), row_spec
    else:
        out_shape = (jax.ShapeDtypeStruct((n_rows, d), F32), jax.ShapeDtypeStruct((n_rows, d), BF16))
        out_specs = (row_spec, row_spec)
    return pl.pallas_call(
        kern,
        out_shape=out_shape,
        grid=(tiles,),
        in_specs=[
            smem_row(lambda i: (0, 0, 0)),
            smem_row(lambda i: (jnp.minimum(i + 1, tiles - 1), 0, 0)),
            pl.BlockSpec(memory_space=pl.ANY),
            pl.BlockSpec((blk, d), lambda i: (i, 0)),
            pl.BlockSpec((blk, TOP_K), lambda i: (i, 0)),
            pl.BlockSpec((1, 1, d), mod),
            pl.BlockSpec((1, d), lambda i: (0, 0)),
            pl.BlockSpec((1, 1, d), mod),
            pl.BlockSpec((1, 1, d), mod),
        ],
        out_specs=out_specs,
        scratch_shapes=[
            pltpu.VMEM((2, TOP_K, blk * VMEM_ROW_PITCH, LANES), F32),
            pltpu.SemaphoreType.DMA((2,)),
        ],
        compiler_params=_params("arbitrary"),
        name="moe_combine",
    )(dest_tiles, dest_tiles, ys, x_tok, gates.T, g2, norm_g.reshape(1, d), norm_shift, norm_scale)


def kernel(x, c, ctx, c_ctx, w_ada, b_ada, norm1_g, norm2_g, w_in, ret_decay, pool_w, pool_scale, sgu_w, sgu_b,
           w_branch, w_out, w_router, router_bias, w_gu, w_down, final_g):
    n_batch, seq, d = x.shape
    ctx_len = ctx.shape[1]
    depth = w_ada.shape[0]
    bw = d // 2
    t_x = n_batch * seq
    t_all = t_x + n_batch * ctx_len
    n_exp = w_router.shape[1]
    assert seq % ROW_TILE == 0 and ctx_len % ROW_TILE == 0 and ctx_len % RET_CHUNK == 0
    assert GRID_W & (GRID_W - 1) == 0 and seq % GRID_W == 0

    pad = (-(n_batch + 1)) % 8
    c_all = jnp.concatenate([c, c_ctx[None, :], jnp.zeros((pad, d), F32)], axis=0)
    ada = _ada(c_all, w_ada, b_ada)
    mods = [[m[:n_batch + 1, None, :] for m in jnp.split(ada[l], 6, axis=-1)] for l in range(depth)]
    no_mod = jnp.zeros((n_batch + 1, 1, d), F32)
    cos_t, sin_t = _rope_tables(seq, ctx_len)
    w_branch_b, w_out_b = w_branch.astype(BF16), w_out.astype(BF16)

    x_main, x_ctx = x.reshape(t_x, d), ctx.reshape(n_batch * ctx_len, d)
    h = _norm(x_main, x_ctx, norm1_g[0], mods[0][0], mods[0][1], seq)
    out = None
    for l in range(depth):
        last = l == depth - 1
        sh1, sc1, g1, sh2, sc2, g2 = mods[l]
        rows_mix = t_x if last else t_all
        p = _matmul_castw(h, w_in, l, t_all, tm=_lane_tile(2304, t_all), tn=1024, name="in_proj")
        lg = jax.nn.log_sigmoid(ret_decay[l].astype(F32)).T
        lg = jnp.broadcast_to(lg[:, :, None], lg.shape + (RET_CHUNK,))
        ret = _retention(p, cos_t, sin_t, lg, n_batch, seq, ctx_len)
        pooled = _pool(p, pool_w[l].astype(BF16), pool_scale[l], 4 * bw, n_batch, seq, ctx_len, not last)
        gated = _sgu(p, sgu_w[l], sgu_b[l], 5 * bw, bw, rows_mix)
        merged = _merge((ret, pooled, gated), w_branch_b, l, p, 7 * bw, rows_mix)
        x_tok = _matmul_residual(merged, w_out_b, l, x_main, x_ctx, g1, rows_mix, seq)
        eid, gates = _norm_router(x_tok, norm2_g[l], sh2, sc2, w_router, router_bias, rows_mix, seq)
        dest_tiles, block_e, n_used, next_e, pad_start, pad_cnt = _route_plan(eid, n_exp, d // LANES)
        x_sorted = _dispatch(x_tok, norm2_g[l], sh2, sc2, dest_tiles, pad_start, pad_cnt, n_used,
                             rows_mix, block_e.shape[0] * ROW_TILE, seq)
        ys = _experts(x_sorted, w_gu, w_down, l, block_e, n_used, next_e)
        if last:
            out = _combine(ys, dest_tiles, x_tok, gates, g2, final_g, no_mod, no_mod, rows_mix, seq, True)
        else:
            x_main, h = _combine(ys, dest_tiles, x_tok, gates, g2, norm1_g[l + 1], mods[l + 1][0], mods[l + 1][1],
                                 rows_mix, seq, False)
            x_ctx = x_main[t_x:]
    return out.reshape(n_batch, seq, d)
```

```python
import functools

import numpy as np
import jax
import jax.numpy as jnp
from jax import lax
from jax.experimental import pallas as pl
from jax.experimental.pallas import tpu as pltpu

F32 = jnp.float32
BF16 = jnp.bfloat16

GRID_W = 64
RET_DK = 128
POOL_WINDOWS = (2, 4, 8, 16)
SGU_CHUNK = 128
N_EXPERT_GROUPS = 4
TOP_K = 2
ROPE_BASE = 10000.0
NORM_EPS = 1e-6

V7X_VMEM_LIMIT_BYTES = 60 * 1024 * 1024
LANES = 128

ROW_TILE = 256
RET_CHUNK = 256
POOL_HALO = 512


def _tile(pref, *sizes):
    t = pref
    while any(s % t for s in sizes):
        t //= 2
    return t


def _lane_tile(cap, size):
    return max(t for t in range(LANES, cap + 1, LANES) if size % t == 0)


def _params(*sem):
    return pltpu.CompilerParams(dimension_semantics=sem, vmem_limit_bytes=V7X_VMEM_LIMIT_BYTES)


def _silu(x):
    return x * jax.nn.sigmoid(x)


def _ada_kernel(c_ref, w_ref, b_ref, o_ref):
    a = _silu(c_ref[...]).astype(BF16)
    o_ref[...] = jnp.dot(a, w_ref[...].astype(BF16), preferred_element_type=F32) + b_ref[...]


def _ada(c_all, w_ada, b_ada, tn=1024):
    depth, d, n = w_ada.shape
    rows = c_all.shape[0]
    return pl.pallas_call(
        _ada_kernel,
        out_shape=jax.ShapeDtypeStruct((depth, rows, n), F32),
        grid=(depth, n // tn),
        in_specs=[
            pl.BlockSpec((rows, d), lambda l, j: (0, 0)),
            pl.BlockSpec((None, d, tn), lambda l, j: (l, 0, j)),
            pl.BlockSpec((None, 1, tn), lambda l, j: (l, 0, j)),
        ],
        out_specs=pl.BlockSpec((None, rows, tn), lambda l, j: (l, 0, j)),
        compiler_params=_params("arbitrary", "arbitrary"),
        name="ada_proj",
    )(c_all, w_ada, b_ada.reshape(depth, 1, n))


def _modulate(x, g, shift, scale):
    r = lax.rsqrt(jnp.mean(x * x, axis=-1, keepdims=True) + NORM_EPS)
    return (x * r) * g * (1.0 + scale) + shift


def _two_source_specs(block, main_tiles, grid_rank):
    main = lambda i: jnp.minimum(i, main_tiles - 1)
    ctx = lambda i: jnp.maximum(i - main_tiles, 0)
    if grid_rank == 1:
        return pl.BlockSpec(block, lambda i: (main(i), 0)), pl.BlockSpec(block, lambda i: (ctx(i), 0))
    return pl.BlockSpec(block, lambda i, j: (main(i), j)), pl.BlockSpec(block, lambda i, j: (ctx(i), j))


def _two_source_pick(i, main_tiles, main_ref, ctx_ref):
    return jnp.where(i < main_tiles, main_ref[...], ctx_ref[...])


def _norm_kernel(xm_ref, xc_ref, g_ref, sh_ref, sc_ref, o_ref, *, main_tiles):
    x = _two_source_pick(pl.program_id(0), main_tiles, xm_ref, xc_ref)
    o_ref[...] = _modulate(x, g_ref[...], sh_ref[0], sc_ref[0]).astype(o_ref.dtype)


def _mod_index(tiles_per_batch, n_batch):
    return lambda i: (jnp.minimum(i // tiles_per_batch, n_batch), 0, 0)


def _norm(x_main, x_ctx, g, shift, scale, seq):
    d = x_main.shape[1]
    n_rows = x_main.shape[0] + x_ctx.shape[0]
    n_batch = shift.shape[0] - 1
    main_tiles = x_main.shape[0] // ROW_TILE
    mod = _mod_index(seq // ROW_TILE, n_batch)
    xm_spec, xc_spec = _two_source_specs((ROW_TILE, d), main_tiles, 1)
    return pl.pallas_call(
        functools.partial(_norm_kernel, main_tiles=main_tiles),
        out_shape=jax.ShapeDtypeStruct((n_rows, d), BF16),
        grid=(n_rows // ROW_TILE,),
        in_specs=[
            xm_spec,
            xc_spec,
            pl.BlockSpec((1, d), lambda i: (0, 0)),
            pl.BlockSpec((1, 1, d), mod),
            pl.BlockSpec((1, 1, d), mod),
        ],
        out_specs=pl.BlockSpec((ROW_TILE, d), lambda i: (i, 0)),
        compiler_params=_params("arbitrary"),
        name="mod_norm",
    )(x_main, x_ctx, g.reshape(1, d), shift, scale)


def _mm_kernel(a_ref, b_ref, o_ref):
    o_ref[...] = jnp.dot(a_ref[...], b_ref[...], preferred_element_type=F32).astype(o_ref.dtype)


def _matmul(a, b, layer, n_rows, tm, tn, out_dtype=BF16, name="matmul"):
    k = a.shape[1]
    n = b.shape[2]
    return pl.pallas_call(
        _mm_kernel,
        out_shape=jax.ShapeDtypeStruct((n_rows, n), out_dtype),
        grid=(n_rows // tm, n // tn),
        in_specs=[
            pl.BlockSpec((tm, k), lambda i, j: (i, 0)),
            pl.BlockSpec((None, k, tn), lambda i, j: (layer, 0, j)),
        ],
        out_specs=pl.BlockSpec((tm, tn), lambda i, j: (i, j)),
        compiler_params=_params("arbitrary", "arbitrary"),
        name=name,
    )(a, b)


def _mm_castw_kernel(a_ref, b_ref, o_ref, wb):
    @pl.when(pl.program_id(1) == 0)
    def _cast():
        rows = wb.shape[0] // 8
        for c in range(8):
            wb[c * rows:(c + 1) * rows, :] = b_ref[c * rows:(c + 1) * rows, :].astype(wb.dtype)

    o_ref[...] = jnp.dot(a_ref[...], wb[...], preferred_element_type=F32).astype(o_ref.dtype)


def _matmul_castw(a, b, layer, n_rows, tm, tn, out_dtype=BF16, name="matmul_castw"):
    k = a.shape[1]
    n = b.shape[2]
    return pl.pallas_call(
        _mm_castw_kernel,
        out_shape=jax.ShapeDtypeStruct((n_rows, n), out_dtype),
        grid=(n // tn, n_rows // tm),
        in_specs=[
            pl.BlockSpec((tm, k), lambda j, i: (i, 0)),
            pl.BlockSpec((None, k, tn), lambda j, i: (layer, 0, j)),
        ],
        out_specs=pl.BlockSpec((tm, tn), lambda j, i: (i, j)),
        scratch_shapes=[pltpu.VMEM((k, tn), a.dtype)],
        compiler_params=_params("arbitrary", "arbitrary"),
        name=name,
    )(a, b)


def _mm_residual_kernel(a_ref, b_ref, g_ref, ng_ref, nsh_ref, nsc_ref, wr_ref, br_ref, *refs, main_tiles):
    *x_refs, o_ref, eid_ref, gate_ref = refs
    y = g_ref[0] * jnp.dot(a_ref[...], b_ref[...], preferred_element_type=F32)
    if len(x_refs) == 1:
        o_ref[...] = x_refs[0][...] + y
    else:
        is_main = pl.program_id(0) < main_tiles

        @pl.when(is_main)
        def _main():
            o_ref[...] = x_refs[0][...] + y

        @pl.when(jnp.logical_not(is_main))
        def _ctx():
            o_ref[...] = x_refs[1][...] + y

    h = _modulate(o_ref[...], ng_ref[...], nsh_ref[0], nsc_ref[0]).astype(BF16)
    _route_tokens(h, wr_ref, br_ref, eid_ref, gate_ref)


def _matmul_residual(a, b, layer, x_main, x_ctx, gate, norm_g, norm_shift, norm_scale, w_router, router_bias,
                     n_rows, seq):
    k = a.shape[1]
    n = b.shape[2]
    n_exp = w_router.shape[1]
    n_batch = gate.shape[0] - 1
    tm = _tile(512, n_rows, seq) if x_ctx is None else _tile(512, n_rows, seq, x_ctx.shape[0])
    tiles_per_batch = seq // tm
    main_tiles = n_batch * tiles_per_batch
    mod = _mod_index(tiles_per_batch, n_batch)
    if x_ctx is None:
        x_specs, x_args = [pl.BlockSpec((tm, n), lambda i: (i, 0))], [x_main]
    else:
        x_specs, x_args = list(_two_source_specs((tm, n), main_tiles, 1)), [x_main, x_ctx]
    return pl.pallas_call(
        functools.partial(_mm_residual_kernel, main_tiles=main_tiles),
        out_shape=(
            jax.ShapeDtypeStruct((n_rows, n), F32),
            jax.ShapeDtypeStruct((TOP_K, n_rows), jnp.int32),
            jax.ShapeDtypeStruct((TOP_K, n_rows), F32),
        ),
        grid=(n_rows // tm,),
        in_specs=[
            pl.BlockSpec((tm, k), lambda i: (i, 0)),
            pl.BlockSpec((None, k, n), lambda i: (layer, 0, 0)),
            pl.BlockSpec((1, 1, n), mod),
            pl.BlockSpec((1, n), lambda i: (0, 0)),
            pl.BlockSpec((1, 1, n), mod),
            pl.BlockSpec((1, 1, n), mod),
            pl.BlockSpec((n_exp, n), lambda i: (0, 0)),
            pl.BlockSpec((n_exp, 1), lambda i: (0, 0)),
        ] + x_specs,
        out_specs=(
            pl.BlockSpec((tm, n), lambda i: (i, 0)),
            pl.BlockSpec((TOP_K, tm), lambda i: (0, i)),
            pl.BlockSpec((TOP_K, tm), lambda i: (0, i)),
        ),
        compiler_params=_params("arbitrary"),
        name="out_proj_residual",
    )(a, b, gate, norm_g.reshape(1, n), norm_shift, norm_scale, w_router.T.astype(BF16),
      router_bias.reshape(n_exp, 1), *x_args)


def _merge_kernel(*refs, n_branch, gate_parts):
    a_refs = refs[:n_branch]
    b_refs = refs[n_branch:2 * n_branch]
    g_refs = refs[2 * n_branch:-1]
    o_ref = refs[-1]
    wide = 4 * LANES
    part = o_ref.shape[1] // gate_parts
    for c in range(o_ref.shape[1] // wide):
        cs = slice(c * wide, (c + 1) * wide)
        gs = slice(c * wide % part, c * wide % part + wide)
        acc = None
        for s in range(n_branch):
            g_ref = g_refs[s * gate_parts + c * wide // part]
            y = (jax.nn.sigmoid(g_ref[:, gs].astype(F32))
                 * jnp.dot(a_refs[s][...], b_refs[s][:, cs], preferred_element_type=F32))
            acc = y if acc is None else acc + y
        o_ref[:, cs] = acc.astype(o_ref.dtype)


def _merge(branches, w_branch, layer, p, gate_col0, n_rows):
    tm = _tile(512, n_rows)
    n_branch = len(branches)
    kb = branches[0].shape[1]
    n = w_branch.shape[3]
    gate_parts = 2
    gw = n // gate_parts
    assert gate_col0 % gw == 0
    a_specs = [pl.BlockSpec((tm, kb), lambda i: (i, 0)) for _ in range(n_branch)]
    b_specs = [pl.BlockSpec((None, None, kb, n), functools.partial(lambda i, s: (layer, s, 0, 0), s=s),
                            pipeline_mode=pl.Buffered(1))
               for s in range(n_branch)]
    g_specs = [
        pl.BlockSpec((tm, gw), functools.partial(lambda i, c: (i, c), c=(gate_col0 + s * n) // gw + part))
        for s in range(n_branch) for part in range(gate_parts)
    ]
    return pl.pallas_call(
        functools.partial(_merge_kernel, n_branch=n_branch, gate_parts=gate_parts),
        out_shape=jax.ShapeDtypeStruct((n_rows, n), BF16),
        grid=(n_rows // tm,),
        in_specs=a_specs + b_specs + g_specs,
        out_specs=pl.BlockSpec((tm, n), lambda i: (i, 0)),
        compiler_params=_params("arbitrary"),
        name="branch_merge",
    )(*branches, *([w_branch] * n_branch), *([p] * (n_branch * gate_parts)))


def _retention_kernel(q_ref, k_ref, v_ref, g_ref, cos_ref, sin_ref, lg_ref, o_ref,
                      sf_all, s_run, dmat, rw, *, n_heads, nc_ctx, nc_x, qscale):
    pss = pl.program_id(1)
    j = pl.program_id(2)
    c = RET_CHUNK
    dk = RET_DK

    @pl.when(jnp.logical_and(pss == 0, j == 0))
    def _tables():
        row = lax.broadcasted_iota(jnp.int32, (c, c), 0)
        col = lax.broadcasted_iota(jnp.int32, (c, c), 1)
        diff = (row - col).astype(F32)
        pos = lax.broadcasted_iota(jnp.int32, (c, dk), 0).astype(F32)
        for h in range(n_heads):
            lgf = lg_ref[h, 0:1, :]
            lgb = lg_ref[h, 1:2, :]
            fwd = jnp.where(diff >= 0, jnp.exp(jnp.maximum(diff, 0.0) * lgf), 0.0)
            bwd = jnp.where(diff <= 0, jnp.exp(jnp.maximum(-diff, 0.0) * lgb), 0.0)
            dmat[h] = fwd + bwd
            lgf1 = lgf[:, :dk]
            lgb1 = lgb[:, :dk]
            rw[h, 0] = jnp.exp((pos + 1.0) * lgf1)
            rw[h, 1] = jnp.exp((c - pos) * lgb1)
            rw[h, 2] = jnp.exp((c - 1.0 - pos) * lgf1)
            rw[h, 3] = jnp.exp(pos * lgb1)

    @pl.when(j == 0)
    def _reset():
        s_run[...] = jnp.zeros_like(s_run)

    cosv = cos_ref[...]
    sinv = sin_ref[...]

    def rope(t):
        return t * cosv + pltpu.roll(t, dk // 2, 1) * sinv

    def state_update(h, kk, vv, direction):
        wv = (vv.astype(F32) * rw[h, 2 + direction]).astype(BF16)
        kv = lax.dot_general(kk, wv, (((0,), (0,)), ((), ())), preferred_element_type=F32)
        decay = jnp.exp(float(c) * lg_ref[h, direction:direction + 1, 0:dk])
        s_run[h] = decay * s_run[h] + kv

    @pl.when(pss == 0)
    def _forward():
        for h in range(n_heads):
            cs = slice(h * dk, (h + 1) * dk)
            sf_all[j, h] = s_run[h]
            kk = rope(k_ref[:, cs].astype(F32)).astype(BF16)
            state_update(h, kk, v_ref[:, cs], 0)

    @pl.when(pss == 1)
    def _backward():
        idx = jnp.where(j < nc_ctx, nc_ctx - 1 - j, nc_ctx + nc_x - 1 - (j - nc_ctx))
        for h in range(n_heads):
            cs = slice(h * dk, (h + 1) * dk)
            kk = rope(k_ref[:, cs].astype(F32)).astype(BF16)
            qq = (rope(q_ref[:, cs].astype(F32)) * qscale).astype(BF16)
            vv = v_ref[:, cs]
            scores = lax.dot_general(qq, kk, (((1,), (1,)), ((), ())), preferred_element_type=F32)
            inner = jnp.dot((scores * dmat[h]).astype(BF16), vv, preferred_element_type=F32)
            states = jnp.concatenate([sf_all[idx, h], s_run[h]], axis=1).astype(BF16)
            cross = jnp.dot(qq, states, preferred_element_type=F32)
            y = inner + rw[h, 0] * cross[:, :dk] + rw[h, 1] * cross[:, dk:]
            mu = jnp.mean(y, axis=-1, keepdims=True)
            yc = y - mu
            var = jnp.mean(yc * yc, axis=-1, keepdims=True)
            yn = yc * lax.rsqrt(var + NORM_EPS)
            o_ref[:, cs] = (yn * _silu(g_ref[:, cs].astype(F32))).astype(o_ref.dtype)
            state_update(h, kk, vv, 1)


def _retention(p, cos_t, sin_t, lg, n_batch, seq, ctx_len):
    bw = lg.shape[0] * RET_DK
    n_heads = lg.shape[0]
    c = RET_CHUNK
    nc_x = seq // c
    nc_ctx = ctx_len // c
    nc = nc_x + nc_ctx
    t = n_batch * (seq + ctx_len)

    def fwd_idx(pss, j):
        bwd = jnp.where(j < nc_ctx, nc_ctx - 1 - j, nc_ctx + nc_x - 1 - (j - nc_ctx))
        return jnp.where(pss == 0, j, bwd)

    def row_block(b, idx):
        return jnp.where(idx < nc_ctx, n_batch * nc_x + b * nc_ctx + idx, b * nc_x + (idx - nc_ctx))

    def kv_map(col):
        return lambda b, pss, j: (row_block(b, fwd_idx(pss, j)), col)

    def qg_map(col):
        return lambda b, pss, j: (row_block(b, fwd_idx(1, jnp.where(pss == 0, 0, j))), col)

    tab_map = lambda b, pss, j: (fwd_idx(pss, j), 0)
    kern = functools.partial(_retention_kernel, n_heads=n_heads, nc_ctx=nc_ctx, nc_x=nc_x,
                             qscale=float(RET_DK) ** -0.5)
    return pl.pallas_call(
        kern,
        out_shape=jax.ShapeDtypeStruct((t, bw), BF16),
        grid=(n_batch, 2, nc),
        in_specs=[
            pl.BlockSpec((c, bw), qg_map(0)),
            pl.BlockSpec((c, bw), kv_map(1)),
            pl.BlockSpec((c, bw), kv_map(2)),
            pl.BlockSpec((c, bw), qg_map(3)),
            pl.BlockSpec((c, RET_DK), tab_map),
            pl.BlockSpec((c, RET_DK), tab_map),
            pl.BlockSpec((n_heads, 2, c), lambda b, pss, j: (0, 0, 0)),
        ],
        out_specs=pl.BlockSpec((c, bw), qg_map(0)),
        scratch_shapes=[
            pltpu.VMEM((nc, n_heads, RET_DK, RET_DK), F32),
            pltpu.VMEM((n_heads, RET_DK, RET_DK), F32),
            pltpu.VMEM((n_heads, c, c), F32),
            pltpu.VMEM((n_heads, 4, c, RET_DK), F32),
        ],
        compiler_params=_params("arbitrary", "arbitrary", "arbitrary"),
        name="retention",
    )(p, p, p, p, cos_t, sin_t, lg)


def _rope_tables(seq, ctx_len):
    t = np.arange(seq)
    rows = (t // GRID_W).astype(np.float32)
    cols = (t % GRID_W).astype(np.float32)
    quarter = RET_DK // 4
    inv = jnp.asarray(ROPE_BASE, F32) ** (-jnp.arange(quarter, dtype=F32) / quarter)
    ang = jnp.concatenate([jnp.asarray(rows)[:, None] * inv, jnp.asarray(cols)[:, None] * inv], axis=-1)
    cos, sin = jnp.cos(ang), jnp.sin(ang)
    cos_t = jnp.concatenate([jnp.ones((ctx_len, RET_DK), F32), jnp.concatenate([cos, cos], axis=-1)], axis=0)
    sin_t = jnp.concatenate([jnp.zeros((ctx_len, RET_DK), F32), jnp.concatenate([-sin, sin], axis=-1)], axis=0)
    return cos_t, sin_t


def _pool_kernel(x_ref, win_ref, w_ref, sc_ref, o_ref, xpad, *, n_tok, halo, grid_rows):
    group = pl.program_id(1)
    slab = ROW_TILE
    if halo:
        xpad[0:halo, :] = jnp.zeros((halo, xpad.shape[1]), xpad.dtype)
        xpad[halo + n_tok:, :] = jnp.zeros((halo, xpad.shape[1]), xpad.dtype)
    xpad[halo:halo + n_tok, :] = x_ref[...]

    def mix(half, lo, width):
        for s in range(n_tok // slab):
            total = jnp.dot(win_ref[:, lo:lo + width], xpad[s * slab + lo:s * slab + lo + width, :],
                            preferred_element_type=F32)
            tok = s * slab + lax.broadcasted_iota(jnp.int32, total.shape, 0)
            if grid_rows is None:
                count = jnp.minimum(tok + half, n_tok) - jnp.maximum(tok - half, 0)
            else:
                col = jnp.bitwise_and(tok, GRID_W - 1)
                row = jnp.right_shift(tok, GRID_W.bit_length() - 1)
                count = ((jnp.minimum(col + half, GRID_W) - jnp.maximum(col - half, 0))
                         * (jnp.minimum(row + half, grid_rows) - jnp.maximum(row - half, 0)))
            mean = total * (1.0 / count.astype(F32))
            d = (mean - x_ref[s * slab:(s + 1) * slab, :].astype(F32)).astype(BF16)
            y = jnp.dot(d, w_ref[...], preferred_element_type=F32) * sc_ref[...]
            o_ref[s * slab:(s + 1) * slab, :] = y.astype(o_ref.dtype)

    if grid_rows is None:
        mix(jnp.left_shift(1, group), 0, win_ref.shape[1])
    else:
        for k, window in enumerate(POOL_WINDOWS):
            reach = window // 2 * GRID_W
            lo = (halo - reach) // LANES * LANES
            hi = -(-(halo + slab + reach) // LANES) * LANES

            @pl.when(group == k)
            def _group(window=window, lo=lo, hi=hi):
                mix(window // 2, lo, hi - lo)


def _pool_windows_2d():
    a = np.arange(ROW_TILE)
    s = np.arange(ROW_TILE + 2 * POOL_HALO)
    ra, ca = a // GRID_W, a % GRID_W
    rs, cs = s // GRID_W - POOL_HALO // GRID_W, s % GRID_W
    out = []
    for w in POOL_WINDOWS:
        h = w // 2
        dr = rs[None, :] - ra[:, None]
        dc = cs[None, :] - ca[:, None]
        out.append(((dr >= -h) & (dr <= h - 1) & (dc >= -h) & (dc <= h - 1)).astype(np.float32))
    return jnp.asarray(np.stack(out), BF16)


def _pool_windows_1d(n):
    a = np.arange(n)
    out = []
    for w in POOL_WINDOWS:
        h = w // 2
        ds = a[None, :] - a[:, None]
        out.append(((ds >= -h) & (ds <= h - 1)).astype(np.float32))
    return jnp.asarray(np.stack(out), BF16)


def _pool(p, pool_w, pool_scale, col0, n_batch, seq, ctx_len, with_ctx):
    n_groups, gw, _ = pool_w.shape
    bw = n_groups * gw
    assert POOL_HALO >= max(POOL_WINDOWS) // 2 * GRID_W and gw == ROW_TILE
    scale = pool_scale.reshape(1, bw)
    cb = col0 // gw

    def call(n_tok, halo, grid_rows, row0_blocks, windows):
        kern = functools.partial(_pool_kernel, n_tok=n_tok, halo=halo, grid_rows=grid_rows)
        return pl.pallas_call(
            kern,
            out_shape=jax.ShapeDtypeStruct((n_batch * n_tok, bw), BF16),
            grid=(n_batch, n_groups),
            in_specs=[
                pl.BlockSpec((n_tok, gw), lambda b, g: (row0_blocks + b, cb + g)),
                pl.BlockSpec((None,) + windows.shape[1:], lambda b, g: (g, 0, 0)),
                pl.BlockSpec((None, gw, gw), lambda b, g: (g, 0, 0)),
                pl.BlockSpec((1, gw), lambda b, g: (0, g)),
            ],
            out_specs=pl.BlockSpec((n_tok, gw), lambda b, g: (b, g)),
            scratch_shapes=[pltpu.VMEM((n_tok + 2 * halo, gw), BF16)],
            compiler_params=_params("arbitrary", "arbitrary"),
            name="pool_mix_ctx" if grid_rows is None else "pool_mix",
        )(p, windows, pool_w, scale)

    out = call(seq, POOL_HALO, seq // GRID_W, 0, _pool_windows_2d())
    if with_ctx:
        out_ctx = call(ctx_len, 0, None, n_batch * seq // ctx_len, _pool_windows_1d(ctx_len))
        out = jnp.concatenate([out, out_ctx], axis=0)
    return out


def _sgu_kernel(u_ref, v_ref, w_ref, b_ref, o_ref, *, n_groups, gw):
    v = v_ref[...].astype(F32)
    mu = jnp.mean(v, axis=-1, keepdims=True)
    vc = v - mu
    var = jnp.mean(vc * vc, axis=-1, keepdims=True)
    vn = (vc * lax.rsqrt(var + NORM_EPS)).astype(BF16)
    for kc in range(ROW_TILE // SGU_CHUNK):
        rs = slice(kc * SGU_CHUNK, (kc + 1) * SGU_CHUNK)
        for g in range(n_groups):
            cs = slice(g * gw, (g + 1) * gw)
            mixed = jnp.dot(w_ref[g], vn[rs, cs], preferred_element_type=F32) + b_ref[:, g:g + 1]
            o_ref[rs, cs] = (u_ref[rs, cs].astype(F32) * mixed).astype(o_ref.dtype)


def _sgu(p, sgu_w, sgu_b, col0, width, n_rows):
    n_groups = sgu_w.shape[0]
    gw = width // n_groups
    cb = col0 // width
    kern = functools.partial(_sgu_kernel, n_groups=n_groups, gw=gw)
    return pl.pallas_call(
        kern,
        out_shape=jax.ShapeDtypeStruct((n_rows, width), BF16),
        grid=(n_rows // ROW_TILE,),
        in_specs=[
            pl.BlockSpec((ROW_TILE, width), lambda i: (i, cb)),
            pl.BlockSpec((ROW_TILE, width), lambda i: (i, cb + 1)),
            pl.BlockSpec(sgu_w.shape, lambda i: (0, 0, 0)),
            pl.BlockSpec((SGU_CHUNK, n_groups), lambda i: (0, 0)),
        ],
        out_specs=pl.BlockSpec((ROW_TILE, width), lambda i: (i, 0)),
        compiler_params=_params("arbitrary"),
        name="spatial_gate",
    )(p, p, sgu_w.astype(BF16), sgu_b.T)


VMEM_ROW_PITCH = 24
DMA_PRIORITIES = 2


def _rows_store(dst_ref, v, pitch):
    rows = v.shape[0]
    for s in range(v.shape[1] // LANES):
        dst_ref[pl.ds(s, rows, stride=pitch), :] = v[:, s * LANES:(s + 1) * LANES].astype(dst_ref.dtype)


def _zero_rows(ref, piece=512):
    for c in range(ref.shape[0] // piece):
        ref[c * piece:(c + 1) * piece, :] = jnp.zeros((piece, ref.shape[1]), ref.dtype)


def _rows_chunk(src_ref, s, rows, pitch):
    return src_ref[pl.ds(s, rows, stride=pitch), :]


def _route_tokens(h, wr_ref, br_ref, eid_ref, gate_ref):
    n_exp = wr_ref.shape[0]
    logits = lax.dot_general(wr_ref[...], h, (((1,), (1,)), ((), ())), preferred_element_type=F32) + br_ref[...]
    e = jnp.exp(logits - jnp.max(logits, axis=0, keepdims=True))
    probs = e / jnp.sum(e, axis=0, keepdims=True)
    per = n_exp // N_EXPERT_GROUPS
    n_tok = probs.shape[1]
    lane_iota = lax.broadcasted_iota(jnp.int32, (per, n_tok), 0)
    best = None
    for grp in range(N_EXPERT_GROUPS):
        pg = probs[grp * per:(grp + 1) * per, :]
        m1 = jnp.max(pg, axis=0, keepdims=True)
        i1 = jnp.min(jnp.where(pg == m1, lane_iota, per), axis=0, keepdims=True)
        rest = jnp.where(lane_iota == i1, -1.0, pg)
        m2 = jnp.max(rest, axis=0, keepdims=True)
        i2 = jnp.min(jnp.where(rest == m2, lane_iota, per), axis=0, keepdims=True)
        cand = (m1 + m2, m1, m2, i1 + grp * per, i2 + grp * per)
        if best is None:
            best = cand
        else:
            take = cand[0] > best[0]
            best = tuple(jnp.where(take, cn, bs) for cn, bs in zip(cand, best))
    _, m1, m2, e1, e2 = best
    denom = m1 + m2
    eid_ref[0:1, :] = e1
    eid_ref[1:2, :] = e2
    gate_ref[0:1, :] = m1 / denom
    gate_ref[1:2, :] = m2 / denom


def _route_plan(eid, n_exp, n_chunks):
    n_tok = eid.shape[1]
    blk = ROW_TILE
    n_assign = TOP_K * n_tok
    tiles = n_tok // blk
    flat_e = eid.reshape(-1)
    onehot = flat_e[:, None] == jnp.arange(n_exp, dtype=jnp.int32)[None, :]
    oh = onehot.astype(BF16).reshape(n_assign // blk, blk, n_exp)
    tril = jnp.tril(jnp.ones((blk, blk), BF16))
    within = jnp.einsum("ij,cjk->cik", tril, oh, preferred_element_type=F32)
    totals = within[:, -1, :]
    offsets = jnp.cumsum(totals, axis=0) - totals
    csum = (within + offsets[:, None, :]).reshape(n_assign, n_exp).astype(jnp.int32)
    counts = csum[-1]
    rank = jnp.sum(jnp.where(onehot, csum, 0), axis=1) - 1
    padded = ((counts + blk - 1) // blk) * blk
    ends = jnp.cumsum(padded)
    start = ends - padded
    dest = (jnp.sum(jnp.where(onehot, start[None, :], 0), axis=1) + rank).astype(jnp.int32)
    dest_tiles = (dest * n_chunks).reshape(TOP_K, tiles, blk).transpose(1, 0, 2).reshape(tiles, 1, TOP_K * blk)
    n_blocks = -(-(n_assign + n_exp * (blk - 1)) // blk)
    block_start = jnp.arange(n_blocks, dtype=jnp.int32) * blk
    block_e = jnp.minimum(jnp.sum((block_start[:, None] >= ends[None, :]).astype(jnp.int32), axis=1), n_exp - 1)
    n_used = ends[-1] // blk
    live = jnp.arange(n_blocks) < n_used
    later = jnp.logical_and(live[None, :], block_e[None, :] > block_e[:, None])
    next_e = jnp.min(jnp.where(later, block_e[None, :], n_exp), axis=1)
    next_e = jnp.where(next_e == n_exp, block_e, next_e)
    i32 = lambda v: v.astype(jnp.int32)
    pad_start = (start + counts) * n_chunks
    pad_cnt = padded - counts
    return dest_tiles, i32(block_e), i32(n_used).reshape(1), i32(next_e), i32(pad_start), i32(pad_cnt)


def _dispatch_kernel(ps_ref, pc_ref, nu_ref, dst_ref, x_ref, g_ref, sh_ref, sc_ref, out_hbm,
                     stage, zbuf, sem, zsem, *, n_chunks):
    i = pl.program_id(0)
    n = pl.num_programs(0)
    blk = x_ref.shape[0]
    pitch = VMEM_ROW_PITCH
    slot = lax.rem(i, 2)
    tile_rows = blk * n_chunks
    n_blocks = out_hbm.shape[0] // tile_rows

    def scatter_wait(buf):
        for _ in range(TOP_K):
            pltpu.make_async_copy(stage.at[buf, pl.ds(0, tile_rows)],
                                  out_hbm.at[pl.ds(0, tile_rows)], sem.at[buf]).wait()

    @pl.when(i == 0)
    def _fill_padding():
        _zero_rows(zbuf)

        def pad_row(e, r):
            dst = pl.multiple_of(ps_ref[e] + r * n_chunks, n_chunks)
            return pltpu.make_async_copy(zbuf.at[pl.ds(0, n_chunks)], out_hbm.at[pl.ds(dst, n_chunks)], zsem)

        def idle_block(b):
            dst = pl.multiple_of(b * tile_rows, tile_rows)
            return pltpu.make_async_copy(zbuf, out_hbm.at[pl.ds(dst, tile_rows)], zsem)

        def for_all(op):
            def per_expert(e, carry):
                def per_row(r, c):
                    op(pad_row(e, r))
                    return c
                return lax.fori_loop(0, pc_ref[e], per_row, carry)
            lax.fori_loop(0, ps_ref.shape[0], per_expert, 0)

            def per_block(b, c):
                op(idle_block(b))
                return c
            lax.fori_loop(nu_ref[0], n_blocks, per_block, 0)

        for_all(lambda c: c.start())
        for_all(lambda c: c.wait())

    @pl.when(i >= 2)
    def _reuse_stage():
        scatter_wait(slot)

    _rows_store(stage.at[slot], _modulate(x_ref[...], g_ref[...], sh_ref[0], sc_ref[0]), pitch)
    for r in range(blk):
        for k in range(TOP_K):
            dst = pl.multiple_of(dst_ref[0, k * blk + r], n_chunks)
            pltpu.make_async_copy(stage.at[slot, pl.ds(r * pitch, n_chunks)],
                                  out_hbm.at[pl.ds(dst, n_chunks)], sem.at[slot]).start(priority=k % DMA_PRIORITIES)

    @pl.when(i == n - 1)
    def _drain():
        scatter_wait(slot)

        @pl.when(i >= 1)
        def _previous():
            scatter_wait(1 - slot)


def _dispatch(x_tok, g, shift, scale, dest_tiles, pad_start, pad_cnt, n_used, n_rows, n_sorted, seq):
    d = x_tok.shape[1]
    n_chunks = d // LANES
    n_batch = shift.shape[0] - 1
    blk = ROW_TILE
    mod = _mod_index(seq // blk, n_batch)
    kern = functools.partial(_dispatch_kernel, n_chunks=n_chunks)
    return pl.pallas_call(
        kern,
        out_shape=jax.ShapeDtypeStruct((n_sorted * n_chunks, LANES), F32),
        grid_spec=pltpu.PrefetchScalarGridSpec(
            num_scalar_prefetch=3,
            grid=(n_rows // blk,),
            in_specs=[
                pl.BlockSpec((None, 1, TOP_K * blk), lambda i, *_: (i, 0, 0), memory_space=pltpu.SMEM),
                pl.BlockSpec((blk, d), lambda i, *_: (i, 0)),
                pl.BlockSpec((1, d), lambda i, *_: (0, 0)),
                pl.BlockSpec((1, 1, d), lambda i, *_: mod(i)),
                pl.BlockSpec((1, 1, d), lambda i, *_: mod(i)),
            ],
            out_specs=pl.BlockSpec(memory_space=pl.ANY),
            scratch_shapes=[
                pltpu.VMEM((2, blk * VMEM_ROW_PITCH, LANES), F32),
                pltpu.VMEM((blk * n_chunks, LANES), F32),
                pltpu.SemaphoreType.DMA((2,)),
                pltpu.SemaphoreType.DMA,
            ],
        ),
        compiler_params=_params("arbitrary"),
        name="moe_dispatch",
    )(pad_start, pad_cnt, n_used, dest_tiles, x_tok, g.reshape(1, d), shift, scale)


def _expert_kernel(be_ref, nu_ref, ne_ref, x_ref, wgu_hbm, wd_hbm, o_ref,
                   xs, act, wgu_b, wd_b, stage_gu, stage_d, wsem, *, layer, n_chunks):
    i = pl.program_id(0)
    n_used = nu_ref[0]
    blk, d = xs.shape
    d_ff = act.shape[1]
    b = i
    live = i < n_used
    expert = be_ref[b]

    def weight_copies(e):
        return (pltpu.make_async_copy(wgu_hbm.at[layer, e], stage_gu, wsem.at[0]),
                pltpu.make_async_copy(wd_hbm.at[layer, e], stage_d, wsem.at[1]))

    @pl.when(i == 0)
    def _prologue():
        for c in weight_copies(expert):
            c.start()

    @pl.when(live)
    def _live():
        @pl.when(jnp.logical_or(b == 0, expert != be_ref[jnp.maximum(b - 1, 0)]))
        def _new_expert():
            for c in weight_copies(expert):
                c.wait()
            rows_gu = d // 8
            for c in range(8):
                rs = slice(c * rows_gu, (c + 1) * rows_gu)
                wgu_b[rs, :] = stage_gu[rs, :].astype(BF16)
            rows_d = d_ff // 4
            for c in range(4):
                rs = slice(c * rows_d, (c + 1) * rows_d)
                wd_b[rs, :] = stage_d[rs, :].astype(BF16)

            @pl.when(ne_ref[b] != expert)
            def _next_weights():
                for c in weight_copies(ne_ref[b]):
                    c.start()

        for s in range(n_chunks):
            xs[:, s * LANES:(s + 1) * LANES] = _rows_chunk(x_ref, s, blk, n_chunks).astype(xs.dtype)
        wide = 2 * LANES
        for c in range(d_ff // wide):
            a = jnp.dot(xs[...], wgu_b[:, c * wide:(c + 1) * wide], preferred_element_type=F32)
            gate = jnp.dot(xs[...], wgu_b[:, d_ff + c * wide:d_ff + (c + 1) * wide], preferred_element_type=F32)
            act[:, c * wide:(c + 1) * wide] = (_silu(a) * gate).astype(act.dtype)
        for c in range(d // (2 * wide)):
            y = jnp.dot(act[...], wd_b[:, c * 2 * wide:(c + 1) * 2 * wide], preferred_element_type=F32)
            for s in range(4):
                o_ref[pl.ds(4 * c + s, blk, stride=n_chunks), :] = y[:, s * LANES:(s + 1) * LANES]

    @pl.when(jnp.logical_not(live))
    def _idle():
        _zero_rows(o_ref)


def _experts(x_sorted, w_gu, w_down, layer, block_e, n_used, next_e):
    _, n_exp, d, two_ff = w_gu.shape
    d_ff = two_ff // 2
    blk = ROW_TILE
    n_chunks = d // LANES
    n_blocks = block_e.shape[0]
    kern = functools.partial(_expert_kernel, layer=layer, n_chunks=n_chunks)
    return pl.pallas_call(
        kern,
        out_shape=jax.ShapeDtypeStruct((n_blocks * blk * n_chunks, LANES), F32),
        grid_spec=pltpu.PrefetchScalarGridSpec(
            num_scalar_prefetch=3,
            grid=(n_blocks,),
            in_specs=[
                pl.BlockSpec((blk * n_chunks, LANES), lambda i, *_: (i, 0)),
                pl.BlockSpec(memory_space=pl.ANY),
                pl.BlockSpec(memory_space=pl.ANY),
            ],
            out_specs=pl.BlockSpec((blk * n_chunks, LANES), lambda i, *_: (i, 0)),
            scratch_shapes=[
                pltpu.VMEM((blk, d), BF16),
                pltpu.VMEM((blk, d_ff), BF16),
                pltpu.VMEM((d, two_ff), BF16),
                pltpu.VMEM((d_ff, d), BF16),
                pltpu.VMEM((d, two_ff), F32),
                pltpu.VMEM((d_ff, d), F32),
                pltpu.SemaphoreType.DMA((2,)),
            ],
        ),
        compiler_params=_params("arbitrary"),
        name="moe_experts",
    )(block_e, n_used, next_e, x_sorted, w_gu, w_down)


def _combine_kernel(dst0_ref, dstn_ref, ys_hbm, x_ref, gate_ref, g2_ref, ng_ref, nsh_ref, nsc_ref, *refs,
                    n_chunks, final_norm):
    if final_norm:
        o_ref, ybuf, sem = refs
        h_ref = o_ref
    else:
        o_ref, h_ref, ybuf, sem = refs
    i = pl.program_id(0)
    n = pl.num_programs(0)
    blk, d = x_ref.shape
    pitch = VMEM_ROW_PITCH
    slot = lax.rem(i, 2)
    tile_rows = blk * n_chunks

    def gather_start(idx_ref, buf):
        for r in range(blk):
            for k in range(TOP_K):
                src = pl.multiple_of(idx_ref[0, k * blk + r], n_chunks)
                pltpu.make_async_copy(ys_hbm.at[pl.ds(src, n_chunks)],
                                      ybuf.at[buf, k, pl.ds(r * pitch, n_chunks)],
                                      sem.at[buf]).start(priority=k % DMA_PRIORITIES)

    @pl.when(i == 0)
    def _prologue():
        gather_start(dst0_ref, 0)

    @pl.when(i + 1 < n)
    def _prefetch():
        gather_start(dstn_ref, 1 - slot)

    for k in range(TOP_K):
        pltpu.make_async_copy(ys_hbm.at[pl.ds(0, tile_rows)],
                              ybuf.at[slot, k, pl.ds(0, tile_rows)], sem.at[slot]).wait()
    gates = [jnp.broadcast_to(gate_ref[:, k:k + 1], (blk, LANES)) for k in range(TOP_K)]
    sq = jnp.zeros((blk, 1), F32)
    for s in range(n_chunks):
        cs = slice(s * LANES, (s + 1) * LANES)
        f = (_rows_chunk(ybuf.at[slot, 0], s, blk, pitch) * gates[0]
             + _rows_chunk(ybuf.at[slot, 1], s, blk, pitch) * gates[1])
        xo = x_ref[:, cs] + g2_ref[0, :, cs] * f
        o_ref[:, cs] = xo
        sq = sq + jnp.sum(xo * xo, axis=-1, keepdims=True)
    r = lax.rsqrt(sq * (1.0 / d) + NORM_EPS)
    for s in range(n_chunks):
        cs = slice(s * LANES, (s + 1) * LANES)
        hn = (o_ref[:, cs] * r) * ng_ref[:, cs] * (1.0 + nsc_ref[0, :, cs]) + nsh_ref[0, :, cs]
        h_ref[:, cs] = hn.astype(h_ref.dtype)


def _combine(ys, dest_tiles, x_tok, gates, g2, norm_g, norm_shift, norm_scale, n_rows, seq, final_norm):
    assert TOP_K == 2
    d = x_tok.shape[1]
    n_batch = g2.shape[0] - 1
    blk = ROW_TILE
    n_chunks = d // LANES
    tiles = n_rows // blk
    mod = _mod_index(seq // blk, n_batch)
    kern = functools.partial(_combine_kernel, n_chunks=n_chunks, final_norm=final_norm)
    smem_row = lambda index: pl.BlockSpec((None, 1, TOP_K * blk), index, memory_space=pltpu.SMEM)
    row_spec = pl.BlockSpec((blk, d), lambda i: (i, 0))
    if final_norm:
        out_shape, out_specs = jax.ShapeDtypeStruct((n_rows, d), F32), row_spec
    else:
        out_shape = (jax.ShapeDtypeStruct((n_rows, d), F32), jax.ShapeDtypeStruct((n_rows, d), BF16))
        out_specs = (row_spec, row_spec)
    return pl.pallas_call(
        kern,
        out_shape=out_shape,
        grid=(tiles,),
        in_specs=[
            smem_row(lambda i: (0, 0, 0)),
            smem_row(lambda i: (jnp.minimum(i + 1, tiles - 1), 0, 0)),
            pl.BlockSpec(memory_space=pl.ANY),
            pl.BlockSpec((blk, d), lambda i: (i, 0)),
            pl.BlockSpec((blk, TOP_K), lambda i: (i, 0)),
            pl.BlockSpec((1, 1, d), mod),
            pl.BlockSpec((1, d), lambda i: (0, 0)),
            pl.BlockSpec((1, 1, d), mod),
            pl.BlockSpec((1, 1, d), mod),
        ],
        out_specs=out_specs,
        scratch_shapes=[
            pltpu.VMEM((2, TOP_K, blk * VMEM_ROW_PITCH, LANES), F32),
            pltpu.SemaphoreType.DMA((2,)),
        ],
        compiler_params=_params("arbitrary"),
        name="moe_combine",
    )(dest_tiles, dest_tiles, ys, x_tok, gates.T, g2, norm_g.reshape(1, d), norm_shift, norm_scale)


def kernel(x, c, ctx, c_ctx, w_ada, b_ada, norm1_g, norm2_g, w_in, ret_decay, pool_w, pool_scale, sgu_w, sgu_b,
           w_branch, w_out, w_router, router_bias, w_gu, w_down, final_g):
    n_batch, seq, d = x.shape
    ctx_len = ctx.shape[1]
    depth = w_ada.shape[0]
    bw = d // 2
    t_x = n_batch * seq
    t_all = t_x + n_batch * ctx_len
    n_exp = w_router.shape[1]
    assert seq % ROW_TILE == 0 and ctx_len % ROW_TILE == 0 and ctx_len % RET_CHUNK == 0
    assert GRID_W & (GRID_W - 1) == 0 and seq % GRID_W == 0

    pad = (-(n_batch + 1)) % 8
    c_all = jnp.concatenate([c, c_ctx[None, :], jnp.zeros((pad, d), F32)], axis=0)
    ada = _ada(c_all, w_ada, b_ada)
    mods = [[m[:n_batch + 1, None, :] for m in jnp.split(ada[l], 6, axis=-1)] for l in range(depth)]
    no_mod = jnp.zeros((n_batch + 1, 1, d), F32)
    cos_t, sin_t = _rope_tables(seq, ctx_len)
    w_branch_b, w_out_b = w_branch.astype(BF16), w_out.astype(BF16)

    x_main, x_ctx = x.reshape(t_x, d), ctx.reshape(n_batch * ctx_len, d)
    h = _norm(x_main, x_ctx, norm1_g[0], mods[0][0], mods[0][1], seq)
    out = None
    for l in range(depth):
        last = l == depth - 1
        sh1, sc1, g1, sh2, sc2, g2 = mods[l]
        rows_mix = t_x if last else t_all
        p = _matmul_castw(h, w_in, l, t_all, tm=_lane_tile(2304, t_all), tn=1024, name="in_proj")
        lg = jax.nn.log_sigmoid(ret_decay[l].astype(F32)).T
        lg = jnp.broadcast_to(lg[:, :, None], lg.shape + (RET_CHUNK,))
        ret = _retention(p, cos_t, sin_t, lg, n_batch, seq, ctx_len)
        pooled = _pool(p, pool_w[l].astype(BF16), pool_scale[l], 4 * bw, n_batch, seq, ctx_len, not last)
        gated = _sgu(p, sgu_w[l], sgu_b[l], 5 * bw, bw, rows_mix)
        merged = _merge((ret, pooled, gated), w_branch_b, l, p, 7 * bw, rows_mix)
        x_tok, eid, gates = _matmul_residual(merged, w_out_b, l, x_main, x_ctx if rows_mix > t_x else None, g1,
                                             norm2_g[l], sh2, sc2, w_router, router_bias, rows_mix, seq)
        dest_tiles, block_e, n_used, next_e, pad_start, pad_cnt = _route_plan(eid, n_exp, d // LANES)
        x_sorted = _dispatch(x_tok, norm2_g[l], sh2, sc2, dest_tiles, pad_start, pad_cnt, n_used,
                             rows_mix, block_e.shape[0] * ROW_TILE, seq)
        ys = _experts(x_sorted, w_gu, w_down, l, block_e, n_used, next_e)
        if last:
            out = _combine(ys, dest_tiles, x_tok, gates, g2, final_g, no_mod, no_mod, rows_mix, seq, True)
        else:
            x_main, h = _combine(ys, dest_tiles, x_tok, gates, g2, norm1_g[l + 1], mods[l + 1][0], mods[l + 1][1],
                                 rows_mix, seq, False)
            x_ctx = x_main[t_x:]
    return out.reshape(n_batch, seq, d)
```

```python
import functools

import numpy as np
import jax
import jax.numpy as jnp
from jax import lax
from jax.experimental import pallas as pl
from jax.experimental.pallas import tpu as pltpu

F32 = jnp.float32
BF16 = jnp.bfloat16

GRID_W = 64
RET_DK = 128
POOL_WINDOWS = (2, 4, 8, 16)
SGU_CHUNK = 128
N_EXPERT_GROUPS = 4
TOP_K = 2
ROPE_BASE = 10000.0
NORM_EPS = 1e-6

V7X_VMEM_LIMIT_BYTES = 60 * 1024 * 1024
LANES = 128

ROW_TILE = 256
RET_CHUNK = 256
POOL_HALO = 512


def _tile(pref, *sizes):
    t = pref
    while any(s % t for s in sizes):
        t //= 2
    return t


def _lane_tile(cap, size):
    return max(t for t in range(LANES, cap + 1, LANES) if size % t == 0)


def _params(*sem):
    return pltpu.CompilerParams(dimension_semantics=sem, vmem_limit_bytes=V7X_VMEM_LIMIT_BYTES)


def _silu(x):
    return x * jax.nn.sigmoid(x)


def _ada_kernel(c_ref, w_ref, b_ref, o_ref):
    a = _silu(c_ref[...]).astype(BF16)
    o_ref[...] = jnp.dot(a, w_ref[...].astype(BF16), preferred_element_type=F32) + b_ref[...]


def _ada(c_all, w_ada, b_ada, tn=1024):
    depth, d, n = w_ada.shape
    rows = c_all.shape[0]
    return pl.pallas_call(
        _ada_kernel,
        out_shape=jax.ShapeDtypeStruct((depth, rows, n), F32),
        grid=(depth, n // tn),
        in_specs=[
            pl.BlockSpec((rows, d), lambda l, j: (0, 0)),
            pl.BlockSpec((None, d, tn), lambda l, j: (l, 0, j)),
            pl.BlockSpec((None, 1, tn), lambda l, j: (l, 0, j)),
        ],
        out_specs=pl.BlockSpec((None, rows, tn), lambda l, j: (l, 0, j)),
        compiler_params=_params("arbitrary", "arbitrary"),
        name="ada_proj",
    )(c_all, w_ada, b_ada.reshape(depth, 1, n))


def _modulate(x, g, shift, scale):
    r = lax.rsqrt(jnp.mean(x * x, axis=-1, keepdims=True) + NORM_EPS)
    return (x * r) * g * (1.0 + scale) + shift


def _two_source_specs(block, main_tiles, grid_rank):
    main = lambda i: jnp.minimum(i, main_tiles - 1)
    ctx = lambda i: jnp.maximum(i - main_tiles, 0)
    if grid_rank == 1:
        return pl.BlockSpec(block, lambda i: (main(i), 0)), pl.BlockSpec(block, lambda i: (ctx(i), 0))
    return pl.BlockSpec(block, lambda i, j: (main(i), j)), pl.BlockSpec(block, lambda i, j: (ctx(i), j))


def _two_source_pick(i, main_tiles, main_ref, ctx_ref):
    return jnp.where(i < main_tiles, main_ref[...], ctx_ref[...])


def _norm_kernel(xm_ref, xc_ref, g_ref, sh_ref, sc_ref, o_ref, *, main_tiles):
    x = _two_source_pick(pl.program_id(0), main_tiles, xm_ref, xc_ref)
    o_ref[...] = _modulate(x, g_ref[...], sh_ref[0], sc_ref[0]).astype(o_ref.dtype)


def _mod_index(tiles_per_batch, n_batch):
    return lambda i: (jnp.minimum(i // tiles_per_batch, n_batch), 0, 0)


def _norm(x_main, x_ctx, g, shift, scale, seq):
    d = x_main.shape[1]
    n_rows = x_main.shape[0] + x_ctx.shape[0]
    n_batch = shift.shape[0] - 1
    main_tiles = x_main.shape[0] // ROW_TILE
    mod = _mod_index(seq // ROW_TILE, n_batch)
    xm_spec, xc_spec = _two_source_specs((ROW_TILE, d), main_tiles, 1)
    return pl.pallas_call(
        functools.partial(_norm_kernel, main_tiles=main_tiles),
        out_shape=jax.ShapeDtypeStruct((n_rows, d), BF16),
        grid=(n_rows // ROW_TILE,),
        in_specs=[
            xm_spec,
            xc_spec,
            pl.BlockSpec((1, d), lambda i: (0, 0)),
            pl.BlockSpec((1, 1, d), mod),
            pl.BlockSpec((1, 1, d), mod),
        ],
        out_specs=pl.BlockSpec((ROW_TILE, d), lambda i: (i, 0)),
        compiler_params=_params("arbitrary"),
        name="mod_norm",
    )(x_main, x_ctx, g.reshape(1, d), shift, scale)


def _mm_kernel(a_ref, b_ref, o_ref):
    o_ref[...] = jnp.dot(a_ref[...], b_ref[...], preferred_element_type=F32).astype(o_ref.dtype)


def _matmul(a, b, layer, n_rows, tm, tn, out_dtype=BF16, name="matmul"):
    k = a.shape[1]
    n = b.shape[2]
    return pl.pallas_call(
        _mm_kernel,
        out_shape=jax.ShapeDtypeStruct((n_rows, n), out_dtype),
        grid=(n_rows // tm, n // tn),
        in_specs=[
            pl.BlockSpec((tm, k), lambda i, j: (i, 0)),
            pl.BlockSpec((None, k, tn), lambda i, j: (layer, 0, j)),
        ],
        out_specs=pl.BlockSpec((tm, tn), lambda i, j: (i, j)),
        compiler_params=_params("arbitrary", "arbitrary"),
        name=name,
    )(a, b)


def _in_proj_kernel(a_ref, b_ref, cos_ref, sin_ref, o_ref, wb, *, rope_tiles, qscale):
    j = pl.program_id(0)

    @pl.when(pl.program_id(1) == 0)
    def _cast():
        rows = wb.shape[0] // 8
        for c in range(8):
            wb[c * rows:(c + 1) * rows, :] = b_ref[c * rows:(c + 1) * rows, :].astype(wb.dtype)

    @pl.when(j >= rope_tiles)
    def _plain():
        wide = o_ref.shape[1] // 2
        for c in range(2):
            cs = slice(c * wide, (c + 1) * wide)
            o_ref[:, cs] = jnp.dot(a_ref[...], wb[:, cs], preferred_element_type=F32).astype(o_ref.dtype)

    @pl.when(j < rope_tiles)
    def _rotated():
        scale = jnp.where(j == 0, qscale, 1.0)
        pair = 2 * RET_DK
        half_rows = o_ref.shape[0] // 2
        for r in range(2):
            rs = slice(r * half_rows, (r + 1) * half_rows)
            cosv = cos_ref[rs, :]
            sinv = sin_ref[rs, :]
            for c in range(o_ref.shape[1] // pair):
                t2 = jnp.dot(a_ref[rs, :], wb[:, c * pair:(c + 1) * pair], preferred_element_type=F32)
                for hh in range(2):
                    t = t2[:, hh * RET_DK:(hh + 1) * RET_DK]
                    hs = slice(c * pair + hh * RET_DK, c * pair + (hh + 1) * RET_DK)
                    rotated = t * cosv + pltpu.roll(t, RET_DK // 2, 1) * sinv
                    o_ref[rs, hs] = (rotated * scale).astype(o_ref.dtype)


def _in_proj(a, b, layer, cos_t, sin_t, n_rows, tm, tn):
    k = a.shape[1]
    n = b.shape[2]
    rope_tiles = 2
    table = lambda j, i: (jnp.where(j < rope_tiles, i, 0), 0)
    kern = functools.partial(_in_proj_kernel, rope_tiles=rope_tiles, qscale=float(RET_DK) ** -0.5)
    return pl.pallas_call(
        kern,
        out_shape=jax.ShapeDtypeStruct((n_rows, n), BF16),
        grid=(n // tn, n_rows // tm),
        in_specs=[
            pl.BlockSpec((tm, k), lambda j, i: (i, 0)),
            pl.BlockSpec((None, k, tn), lambda j, i: (layer, 0, j)),
            pl.BlockSpec((tm, RET_DK), table),
            pl.BlockSpec((tm, RET_DK), table),
        ],
        out_specs=pl.BlockSpec((tm, tn), lambda j, i: (i, j)),
        scratch_shapes=[pltpu.VMEM((k, tn), a.dtype)],
        compiler_params=_params("arbitrary", "arbitrary"),
        name="in_proj",
    )(a, b, cos_t, sin_t)


def _mm_residual_kernel(a_ref, b_ref, g_ref, ng_ref, nsh_ref, nsc_ref, wr_ref, br_ref, *refs, main_tiles):
    *x_refs, o_ref, eid_ref, gate_ref = refs
    y = g_ref[0] * jnp.dot(a_ref[...], b_ref[...], preferred_element_type=F32)
    if len(x_refs) == 1:
        o_ref[...] = x_refs[0][...] + y
    else:
        is_main = pl.program_id(0) < main_tiles

        @pl.when(is_main)
        def _main():
            o_ref[...] = x_refs[0][...] + y

        @pl.when(jnp.logical_not(is_main))
        def _ctx():
            o_ref[...] = x_refs[1][...] + y

    h = _modulate(o_ref[...], ng_ref[...], nsh_ref[0], nsc_ref[0]).astype(BF16)
    _route_tokens(h, wr_ref, br_ref, eid_ref, gate_ref)


def _matmul_residual(a, b, layer, x_main, x_ctx, gate, norm_g, norm_shift, norm_scale, w_router, router_bias,
                     n_rows, seq):
    k = a.shape[1]
    n = b.shape[2]
    n_exp = w_router.shape[1]
    n_batch = gate.shape[0] - 1
    tm = _tile(512, n_rows, seq) if x_ctx is None else _tile(512, n_rows, seq, x_ctx.shape[0])
    tiles_per_batch = seq // tm
    main_tiles = n_batch * tiles_per_batch
    mod = _mod_index(tiles_per_batch, n_batch)
    if x_ctx is None:
        x_specs, x_args = [pl.BlockSpec((tm, n), lambda i: (i, 0))], [x_main]
    else:
        x_specs, x_args = list(_two_source_specs((tm, n), main_tiles, 1)), [x_main, x_ctx]
    return pl.pallas_call(
        functools.partial(_mm_residual_kernel, main_tiles=main_tiles),
        out_shape=(
            jax.ShapeDtypeStruct((n_rows, n), F32),
            jax.ShapeDtypeStruct((TOP_K, n_rows), jnp.int32),
            jax.ShapeDtypeStruct((TOP_K, n_rows), F32),
        ),
        grid=(n_rows // tm,),
        in_specs=[
            pl.BlockSpec((tm, k), lambda i: (i, 0)),
            pl.BlockSpec((None, k, n), lambda i: (layer, 0, 0)),
            pl.BlockSpec((1, 1, n), mod),
            pl.BlockSpec((1, n), lambda i: (0, 0)),
            pl.BlockSpec((1, 1, n), mod),
            pl.BlockSpec((1, 1, n), mod),
            pl.BlockSpec((n_exp, n), lambda i: (0, 0)),
            pl.BlockSpec((n_exp, 1), lambda i: (0, 0)),
        ] + x_specs,
        out_specs=(
            pl.BlockSpec((tm, n), lambda i: (i, 0)),
            pl.BlockSpec((TOP_K, tm), lambda i: (0, i)),
            pl.BlockSpec((TOP_K, tm), lambda i: (0, i)),
        ),
        compiler_params=_params("arbitrary"),
        name="out_proj_residual",
    )(a, b, gate, norm_g.reshape(1, n), norm_shift, norm_scale, w_router.T.astype(BF16),
      router_bias.reshape(n_exp, 1), *x_args)


def _merge_kernel(*refs, n_branch, gate_parts):
    a_refs = refs[:n_branch]
    b_refs = refs[n_branch:2 * n_branch]
    g_refs = refs[2 * n_branch:-1]
    o_ref = refs[-1]
    wide = 4 * LANES
    part = o_ref.shape[1] // gate_parts
    for c in range(o_ref.shape[1] // wide):
        cs = slice(c * wide, (c + 1) * wide)
        gs = slice(c * wide % part, c * wide % part + wide)
        acc = None
        for s in range(n_branch):
            g_ref = g_refs[s * gate_parts + c * wide // part]
            y = (jax.nn.sigmoid(g_ref[:, gs].astype(F32))
                 * jnp.dot(a_refs[s][...], b_refs[s][:, cs], preferred_element_type=F32))
            acc = y if acc is None else acc + y
        o_ref[:, cs] = acc.astype(o_ref.dtype)


def _merge(branches, w_branch, layer, p, gate_col0, n_rows):
    tm = _tile(512, n_rows)
    n_branch = len(branches)
    kb = branches[0].shape[1]
    n = w_branch.shape[3]
    gate_parts = 2
    gw = n // gate_parts
    assert gate_col0 % gw == 0
    a_specs = [pl.BlockSpec((tm, kb), lambda i: (i, 0)) for _ in range(n_branch)]
    b_specs = [pl.BlockSpec((None, None, kb, n), functools.partial(lambda i, s: (layer, s, 0, 0), s=s),
                            pipeline_mode=pl.Buffered(1))
               for s in range(n_branch)]
    g_specs = [
        pl.BlockSpec((tm, gw), functools.partial(lambda i, c: (i, c), c=(gate_col0 + s * n) // gw + part))
        for s in range(n_branch) for part in range(gate_parts)
    ]
    return pl.pallas_call(
        functools.partial(_merge_kernel, n_branch=n_branch, gate_parts=gate_parts),
        out_shape=jax.ShapeDtypeStruct((n_rows, n), BF16),
        grid=(n_rows // tm,),
        in_specs=a_specs + b_specs + g_specs,
        out_specs=pl.BlockSpec((tm, n), lambda i: (i, 0)),
        compiler_params=_params("arbitrary"),
        name="branch_merge",
    )(*branches, *([w_branch] * n_branch), *([p] * (n_branch * gate_parts)))


def _retention_kernel(q_ref, k_ref, v_ref, g_ref, lg_ref, o_ref,
                      sf_all, s_run, dmat, rw, *, n_heads, nc_ctx, nc_x):
    pss = pl.program_id(1)
    j = pl.program_id(2)
    c = RET_CHUNK
    dk = RET_DK

    @pl.when(jnp.logical_and(pss == 0, j == 0))
    def _tables():
        row = lax.broadcasted_iota(jnp.int32, (c, c), 0)
        col = lax.broadcasted_iota(jnp.int32, (c, c), 1)
        diff = (row - col).astype(F32)
        pos = lax.broadcasted_iota(jnp.int32, (c, dk), 0).astype(F32)
        for h in range(n_heads):
            lgf = lg_ref[h, 0:1, :]
            lgb = lg_ref[h, 1:2, :]
            fwd = jnp.where(diff >= 0, jnp.exp(jnp.maximum(diff, 0.0) * lgf), 0.0)
            bwd = jnp.where(diff <= 0, jnp.exp(jnp.maximum(-diff, 0.0) * lgb), 0.0)
            dmat[h] = fwd + bwd
            lgf1 = lgf[:, :dk]
            lgb1 = lgb[:, :dk]
            rw[h, 0] = jnp.exp((pos + 1.0) * lgf1)
            rw[h, 1] = jnp.exp((c - pos) * lgb1)
            rw[h, 2] = jnp.exp((c - 1.0 - pos) * lgf1)
            rw[h, 3] = jnp.exp(pos * lgb1)

    @pl.when(j == 0)
    def _reset():
        s_run[...] = jnp.zeros_like(s_run)

    def state_update(h, kk, vv, direction):
        wv = (vv.astype(F32) * rw[h, 2 + direction]).astype(BF16)
        kv = lax.dot_general(kk, wv, (((0,), (0,)), ((), ())), preferred_element_type=F32)
        decay = jnp.exp(float(c) * lg_ref[h, direction:direction + 1, 0:dk])
        s_run[h] = decay * s_run[h] + kv

    @pl.when(pss == 0)
    def _forward():
        for h in range(n_heads):
            cs = slice(h * dk, (h + 1) * dk)
            sf_all[j, h] = s_run[h]
            state_update(h, k_ref[:, cs], v_ref[:, cs], 0)

    @pl.when(pss == 1)
    def _backward():
        idx = jnp.where(j < nc_ctx, nc_ctx - 1 - j, nc_ctx + nc_x - 1 - (j - nc_ctx))
        for h in range(n_heads):
            cs = slice(h * dk, (h + 1) * dk)
            kk = k_ref[:, cs]
            qq = q_ref[:, cs]
            vv = v_ref[:, cs]
            scores = lax.dot_general(qq, kk, (((1,), (1,)), ((), ())), preferred_element_type=F32)
            inner = jnp.dot((scores * dmat[h]).astype(BF16), vv, preferred_element_type=F32)
            states = jnp.concatenate([sf_all[idx, h], s_run[h]], axis=1).astype(BF16)
            cross = jnp.dot(qq, states, preferred_element_type=F32)
            y = inner + rw[h, 0] * cross[:, :dk] + rw[h, 1] * cross[:, dk:]
            mu = jnp.mean(y, axis=-1, keepdims=True)
            yc = y - mu
            var = jnp.mean(yc * yc, axis=-1, keepdims=True)
            yn = yc * lax.rsqrt(var + NORM_EPS)
            o_ref[:, cs] = (yn * _silu(g_ref[:, cs].astype(F32))).astype(o_ref.dtype)
            state_update(h, kk, vv, 1)


def _retention(p, lg, n_batch, seq, ctx_len):
    bw = lg.shape[0] * RET_DK
    n_heads = lg.shape[0]
    c = RET_CHUNK
    nc_x = seq // c
    nc_ctx = ctx_len // c
    nc = nc_x + nc_ctx
    t = n_batch * (seq + ctx_len)

    def fwd_idx(pss, j):
        bwd = jnp.where(j < nc_ctx, nc_ctx - 1 - j, nc_ctx + nc_x - 1 - (j - nc_ctx))
        return jnp.where(pss == 0, j, bwd)

    def row_block(b, idx):
        return jnp.where(idx < nc_ctx, n_batch * nc_x + b * nc_ctx + idx, b * nc_x + (idx - nc_ctx))

    def kv_map(col):
        return lambda b, pss, j: (row_block(b, fwd_idx(pss, j)), col)

    def qg_map(col):
        return lambda b, pss, j: (row_block(b, fwd_idx(1, jnp.where(pss == 0, 0, j))), col)

    kern = functools.partial(_retention_kernel, n_heads=n_heads, nc_ctx=nc_ctx, nc_x=nc_x)
    return pl.pallas_call(
        kern,
        out_shape=jax.ShapeDtypeStruct((t, bw), BF16),
        grid=(n_batch, 2, nc),
        in_specs=[
            pl.BlockSpec((c, bw), qg_map(0)),
            pl.BlockSpec((c, bw), kv_map(1)),
            pl.BlockSpec((c, bw), kv_map(2)),
            pl.BlockSpec((c, bw), qg_map(3)),
            pl.BlockSpec((n_heads, 2, c), lambda b, pss, j: (0, 0, 0)),
        ],
        out_specs=pl.BlockSpec((c, bw), qg_map(0)),
        scratch_shapes=[
            pltpu.VMEM((nc, n_heads, RET_DK, RET_DK), F32),
            pltpu.VMEM((n_heads, RET_DK, RET_DK), F32),
            pltpu.VMEM((n_heads, c, c), F32),
            pltpu.VMEM((n_heads, 4, c, RET_DK), F32),
        ],
        compiler_params=_params("arbitrary", "arbitrary", "arbitrary"),
        name="retention",
    )(p, p, p, p, lg)


def _rope_tables(n_batch, seq, ctx_len):
    t = np.arange(seq)
    rows = (t // GRID_W).astype(np.float32)
    cols = (t % GRID_W).astype(np.float32)
    quarter = RET_DK // 4
    inv = jnp.asarray(ROPE_BASE, F32) ** (-jnp.arange(quarter, dtype=F32) / quarter)
    ang = jnp.concatenate([jnp.asarray(rows)[:, None] * inv, jnp.asarray(cols)[:, None] * inv], axis=-1)
    cos, sin = jnp.cos(ang), jnp.sin(ang)
    cos_x = jnp.tile(jnp.concatenate([cos, cos], axis=-1), (n_batch, 1))
    sin_x = jnp.tile(jnp.concatenate([-sin, sin], axis=-1), (n_batch, 1))
    cos_t = jnp.concatenate([cos_x, jnp.ones((n_batch * ctx_len, RET_DK), F32)], axis=0)
    sin_t = jnp.concatenate([sin_x, jnp.zeros((n_batch * ctx_len, RET_DK), F32)], axis=0)
    return cos_t, sin_t


def _pool_kernel(x_ref, win_ref, w_ref, sc_ref, o_ref, xpad, *, n_tok, halo, grid_rows):
    group = pl.program_id(1)
    slab = ROW_TILE
    if halo:
        xpad[0:halo, :] = jnp.zeros((halo, xpad.shape[1]), xpad.dtype)
        xpad[halo + n_tok:, :] = jnp.zeros((halo, xpad.shape[1]), xpad.dtype)
    xpad[halo:halo + n_tok, :] = x_ref[...]

    def mix(half, lo, width):
        for s in range(n_tok // slab):
            total = jnp.dot(win_ref[:, lo:lo + width], xpad[s * slab + lo:s * slab + lo + width, :],
                            preferred_element_type=F32)
            tok = s * slab + lax.broadcasted_iota(jnp.int32, total.shape, 0)
            if grid_rows is None:
                count = jnp.minimum(tok + half, n_tok) - jnp.maximum(tok - half, 0)
            else:
                col = jnp.bitwise_and(tok, GRID_W - 1)
                row = jnp.right_shift(tok, GRID_W.bit_length() - 1)
                count = ((jnp.minimum(col + half, GRID_W) - jnp.maximum(col - half, 0))
                         * (jnp.minimum(row + half, grid_rows) - jnp.maximum(row - half, 0)))
            mean = total * (1.0 / count.astype(F32))
            d = (mean - x_ref[s * slab:(s + 1) * slab, :].astype(F32)).astype(BF16)
            y = jnp.dot(d, w_ref[...], preferred_element_type=F32) * sc_ref[...]
            o_ref[s * slab:(s + 1) * slab, :] = y.astype(o_ref.dtype)

    if grid_rows is None:
        mix(jnp.left_shift(1, group), 0, win_ref.shape[1])
    else:
        for k, window in enumerate(POOL_WINDOWS):
            reach = window // 2 * GRID_W
            lo = (halo - reach) // LANES * LANES
            hi = -(-(halo + slab + reach) // LANES) * LANES

            @pl.when(group == k)
            def _group(window=window, lo=lo, hi=hi):
                mix(window // 2, lo, hi - lo)


def _pool_windows_2d():
    a = np.arange(ROW_TILE)
    s = np.arange(ROW_TILE + 2 * POOL_HALO)
    ra, ca = a // GRID_W, a % GRID_W
    rs, cs = s // GRID_W - POOL_HALO // GRID_W, s % GRID_W
    out = []
    for w in POOL_WINDOWS:
        h = w // 2
        dr = rs[None, :] - ra[:, None]
        dc = cs[None, :] - ca[:, None]
        out.append(((dr >= -h) & (dr <= h - 1) & (dc >= -h) & (dc <= h - 1)).astype(np.float32))
    return jnp.asarray(np.stack(out), BF16)


def _pool_windows_1d(n):
    a = np.arange(n)
    out = []
    for w in POOL_WINDOWS:
        h = w // 2
        ds = a[None, :] - a[:, None]
        out.append(((ds >= -h) & (ds <= h - 1)).astype(np.float32))
    return jnp.asarray(np.stack(out), BF16)


def _pool(p, pool_w, pool_scale, col0, n_batch, seq, ctx_len, with_ctx):
    n_groups, gw, _ = pool_w.shape
    bw = n_groups * gw
    assert POOL_HALO >= max(POOL_WINDOWS) // 2 * GRID_W and gw == ROW_TILE
    scale = pool_scale.reshape(1, bw)
    cb = col0 // gw

    def call(n_tok, halo, grid_rows, row0_blocks, windows):
        kern = functools.partial(_pool_kernel, n_tok=n_tok, halo=halo, grid_rows=grid_rows)
        return pl.pallas_call(
            kern,
            out_shape=jax.ShapeDtypeStruct((n_batch * n_tok, bw), BF16),
            grid=(n_batch, n_groups),
            in_specs=[
                pl.BlockSpec((n_tok, gw), lambda b, g: (row0_blocks + b, cb + g)),
                pl.BlockSpec((None,) + windows.shape[1:], lambda b, g: (g, 0, 0)),
                pl.BlockSpec((None, gw, gw), lambda b, g: (g, 0, 0)),
                pl.BlockSpec((1, gw), lambda b, g: (0, g)),
            ],
            out_specs=pl.BlockSpec((n_tok, gw), lambda b, g: (b, g)),
            scratch_shapes=[pltpu.VMEM((n_tok + 2 * halo, gw), BF16)],
            compiler_params=_params("arbitrary", "arbitrary"),
            name="pool_mix_ctx" if grid_rows is None else "pool_mix",
        )(p, windows, pool_w, scale)

    out = call(seq, POOL_HALO, seq // GRID_W, 0, _pool_windows_2d())
    if with_ctx:
        out_ctx = call(ctx_len, 0, None, n_batch * seq // ctx_len, _pool_windows_1d(ctx_len))
        out = jnp.concatenate([out, out_ctx], axis=0)
    return out


def _sgu_kernel(u_ref, v_ref, w_ref, b_ref, o_ref, *, n_groups, gw):
    v = v_ref[...].astype(F32)
    mu = jnp.mean(v, axis=-1, keepdims=True)
    vc = v - mu
    var = jnp.mean(vc * vc, axis=-1, keepdims=True)
    vn = (vc * lax.rsqrt(var + NORM_EPS)).astype(BF16)
    for kc in range(ROW_TILE // SGU_CHUNK):
        rs = slice(kc * SGU_CHUNK, (kc + 1) * SGU_CHUNK)
        for g in range(n_groups):
            cs = slice(g * gw, (g + 1) * gw)
            mixed = jnp.dot(w_ref[g], vn[rs, cs], preferred_element_type=F32) + b_ref[:, g:g + 1]
            o_ref[rs, cs] = (u_ref[rs, cs].astype(F32) * mixed).astype(o_ref.dtype)


def _sgu(p, sgu_w, sgu_b, col0, width, n_rows):
    n_groups = sgu_w.shape[0]
    gw = width // n_groups
    cb = col0 // width
    kern = functools.partial(_sgu_kernel, n_groups=n_groups, gw=gw)
    return pl.pallas_call(
        kern,
        out_shape=jax.ShapeDtypeStruct((n_rows, width), BF16),
        grid=(n_rows // ROW_TILE,),
        in_specs=[
            pl.BlockSpec((ROW_TILE, width), lambda i: (i, cb)),
            pl.BlockSpec((ROW_TILE, width), lambda i: (i, cb + 1)),
            pl.BlockSpec(sgu_w.shape, lambda i: (0, 0, 0)),
            pl.BlockSpec((SGU_CHUNK, n_groups), lambda i: (0, 0)),
        ],
        out_specs=pl.BlockSpec((ROW_TILE, width), lambda i: (i, 0)),
        compiler_params=_params("arbitrary"),
        name="spatial_gate",
    )(p, p, sgu_w.astype(BF16), sgu_b.T)


VMEM_ROW_PITCH = 24
DMA_PRIORITIES = 2


def _rows_store(dst_ref, v, pitch):
    rows = v.shape[0]
    for s in range(v.shape[1] // LANES):
        dst_ref[pl.ds(s, rows, stride=pitch), :] = v[:, s * LANES:(s + 1) * LANES].astype(dst_ref.dtype)


def _zero_rows(ref, piece=512):
    for c in range(ref.shape[0] // piece):
        ref[c * piece:(c + 1) * piece, :] = jnp.zeros((piece, ref.shape[1]), ref.dtype)


def _rows_chunk(src_ref, s, rows, pitch):
    return src_ref[pl.ds(s, rows, stride=pitch), :]


def _route_tokens(h, wr_ref, br_ref, eid_ref, gate_ref):
    n_exp = wr_ref.shape[0]
    logits = lax.dot_general(wr_ref[...], h, (((1,), (1,)), ((), ())), preferred_element_type=F32) + br_ref[...]
    e = jnp.exp(logits - jnp.max(logits, axis=0, keepdims=True))
    probs = e / jnp.sum(e, axis=0, keepdims=True)
    per = n_exp // N_EXPERT_GROUPS
    n_tok = probs.shape[1]
    lane_iota = lax.broadcasted_iota(jnp.int32, (per, n_tok), 0)
    best = None
    for grp in range(N_EXPERT_GROUPS):
        pg = probs[grp * per:(grp + 1) * per, :]
        m1 = jnp.max(pg, axis=0, keepdims=True)
        i1 = jnp.min(jnp.where(pg == m1, lane_iota, per), axis=0, keepdims=True)
        rest = jnp.where(lane_iota == i1, -1.0, pg)
        m2 = jnp.max(rest, axis=0, keepdims=True)
        i2 = jnp.min(jnp.where(rest == m2, lane_iota, per), axis=0, keepdims=True)
        cand = (m1 + m2, m1, m2, i1 + grp * per, i2 + grp * per)
        if best is None:
            best = cand
        else:
            take = cand[0] > best[0]
            best = tuple(jnp.where(take, cn, bs) for cn, bs in zip(cand, best))
    _, m1, m2, e1, e2 = best
    denom = m1 + m2
    eid_ref[0:1, :] = e1
    eid_ref[1:2, :] = e2
    gate_ref[0:1, :] = m1 / denom
    gate_ref[1:2, :] = m2 / denom


def _route_plan(eid, n_exp, n_chunks):
    n_tok = eid.shape[1]
    blk = ROW_TILE
    n_assign = TOP_K * n_tok
    tiles = n_tok // blk
    flat_e = eid.reshape(-1)
    onehot = flat_e[:, None] == jnp.arange(n_exp, dtype=jnp.int32)[None, :]
    oh = onehot.astype(BF16).reshape(n_assign // blk, blk, n_exp)
    tril = jnp.tril(jnp.ones((blk, blk), BF16))
    within = jnp.einsum("ij,cjk->cik", tril, oh, preferred_element_type=F32)
    totals = within[:, -1, :]
    offsets = jnp.cumsum(totals, axis=0) - totals
    csum = (within + offsets[:, None, :]).reshape(n_assign, n_exp).astype(jnp.int32)
    counts = csum[-1]
    rank = jnp.sum(jnp.where(onehot, csum, 0), axis=1) - 1
    padded = ((counts + blk - 1) // blk) * blk
    ends = jnp.cumsum(padded)
    start = ends - padded
    dest = (jnp.sum(jnp.where(onehot, start[None, :], 0), axis=1) + rank).astype(jnp.int32)
    dest_tiles = (dest * n_chunks).reshape(TOP_K, tiles, blk).transpose(1, 0, 2).reshape(tiles, 1, TOP_K * blk)
    n_blocks = -(-(n_assign + n_exp * (blk - 1)) // blk)
    block_start = jnp.arange(n_blocks, dtype=jnp.int32) * blk
    block_e = jnp.minimum(jnp.sum((block_start[:, None] >= ends[None, :]).astype(jnp.int32), axis=1), n_exp - 1)
    n_used = ends[-1] // blk
    live = jnp.arange(n_blocks) < n_used
    later = jnp.logical_and(live[None, :], block_e[None, :] > block_e[:, None])
    next_e = jnp.min(jnp.where(later, block_e[None, :], n_exp), axis=1)
    next_e = jnp.where(next_e == n_exp, block_e, next_e)
    i32 = lambda v: v.astype(jnp.int32)
    pad_start = (start + counts) * n_chunks
    pad_cnt = padded - counts
    return dest_tiles, i32(block_e), i32(n_used).reshape(1), i32(next_e), i32(pad_start), i32(pad_cnt)


def _dispatch_kernel(ps_ref, pc_ref, nu_ref, dst_ref, x_ref, g_ref, sh_ref, sc_ref, out_hbm,
                     stage, zbuf, sem, zsem, *, n_chunks):
    i = pl.program_id(0)
    n = pl.num_programs(0)
    blk = x_ref.shape[0]
    pitch = VMEM_ROW_PITCH
    slot = lax.rem(i, 2)
    tile_rows = blk * n_chunks
    n_blocks = out_hbm.shape[0] // tile_rows

    def scatter_wait(buf):
        for _ in range(TOP_K):
            pltpu.make_async_copy(stage.at[buf, pl.ds(0, tile_rows)],
                                  out_hbm.at[pl.ds(0, tile_rows)], sem.at[buf]).wait()

    @pl.when(i == 0)
    def _fill_padding():
        _zero_rows(zbuf)

        def pad_row(e, r):
            dst = pl.multiple_of(ps_ref[e] + r * n_chunks, n_chunks)
            return pltpu.make_async_copy(zbuf.at[pl.ds(0, n_chunks)], out_hbm.at[pl.ds(dst, n_chunks)], zsem)

        def idle_block(b):
            dst = pl.multiple_of(b * tile_rows, tile_rows)
            return pltpu.make_async_copy(zbuf, out_hbm.at[pl.ds(dst, tile_rows)], zsem)

        def for_all(op):
            def per_expert(e, carry):
                def per_row(r, c):
                    op(pad_row(e, r))
                    return c
                return lax.fori_loop(0, pc_ref[e], per_row, carry)
            lax.fori_loop(0, ps_ref.shape[0], per_expert, 0)

            def per_block(b, c):
                op(idle_block(b))
                return c
            lax.fori_loop(nu_ref[0], n_blocks, per_block, 0)

        for_all(lambda c: c.start())
        for_all(lambda c: c.wait())

    @pl.when(i >= 2)
    def _reuse_stage():
        scatter_wait(slot)

    _rows_store(stage.at[slot], _modulate(x_ref[...], g_ref[...], sh_ref[0], sc_ref[0]), pitch)
    for r in range(blk):
        for k in range(TOP_K):
            dst = pl.multiple_of(dst_ref[0, k * blk + r], n_chunks)
            pltpu.make_async_copy(stage.at[slot, pl.ds(r * pitch, n_chunks)],
                                  out_hbm.at[pl.ds(dst, n_chunks)], sem.at[slot]).start(priority=k % DMA_PRIORITIES)

    @pl.when(i == n - 1)
    def _drain():
        scatter_wait(slot)

        @pl.when(i >= 1)
        def _previous():
            scatter_wait(1 - slot)


def _dispatch(x_tok, g, shift, scale, dest_tiles, pad_start, pad_cnt, n_used, n_rows, n_sorted, seq):
    d = x_tok.shape[1]
    n_chunks = d // LANES
    n_batch = shift.shape[0] - 1
    blk = ROW_TILE
    mod = _mod_index(seq // blk, n_batch)
    kern = functools.partial(_dispatch_kernel, n_chunks=n_chunks)
    return pl.pallas_call(
        kern,
        out_shape=jax.ShapeDtypeStruct((n_sorted * n_chunks, LANES), F32),
        grid_spec=pltpu.PrefetchScalarGridSpec(
            num_scalar_prefetch=3,
            grid=(n_rows // blk,),
            in_specs=[
                pl.BlockSpec((None, 1, TOP_K * blk), lambda i, *_: (i, 0, 0), memory_space=pltpu.SMEM),
                pl.BlockSpec((blk, d), lambda i, *_: (i, 0)),
                pl.BlockSpec((1, d), lambda i, *_: (0, 0)),
                pl.BlockSpec((1, 1, d), lambda i, *_: mod(i)),
                pl.BlockSpec((1, 1, d), lambda i, *_: mod(i)),
            ],
            out_specs=pl.BlockSpec(memory_space=pl.ANY),
            scratch_shapes=[
                pltpu.VMEM((2, blk * VMEM_ROW_PITCH, LANES), F32),
                pltpu.VMEM((blk * n_chunks, LANES), F32),
                pltpu.SemaphoreType.DMA((2,)),
                pltpu.SemaphoreType.DMA,
            ],
        ),
        compiler_params=_params("arbitrary"),
        name="moe_dispatch",
    )(pad_start, pad_cnt, n_used, dest_tiles, x_tok, g.reshape(1, d), shift, scale)


def _expert_kernel(be_ref, nu_ref, ne_ref, x_ref, wgu_hbm, wd_hbm, o_ref,
                   xs, act, wgu_b, wd_b, stage_gu, stage_d, wsem, *, layer, n_chunks):
    i = pl.program_id(0)
    n_used = nu_ref[0]
    blk, d = xs.shape
    d_ff = act.shape[1]
    b = i
    live = i < n_used
    expert = be_ref[b]

    def weight_copies(e):
        return (pltpu.make_async_copy(wgu_hbm.at[layer, e], stage_gu, wsem.at[0]),
                pltpu.make_async_copy(wd_hbm.at[layer, e], stage_d, wsem.at[1]))

    @pl.when(i == 0)
    def _prologue():
        for c in weight_copies(expert):
            c.start()

    @pl.when(live)
    def _live():
        @pl.when(jnp.logical_or(b == 0, expert != be_ref[jnp.maximum(b - 1, 0)]))
        def _new_expert():
            for c in weight_copies(expert):
                c.wait()
            rows_gu = d // 8
            for c in range(8):
                rs = slice(c * rows_gu, (c + 1) * rows_gu)
                wgu_b[rs, :] = stage_gu[rs, :].astype(BF16)
            rows_d = d_ff // 4
            for c in range(4):
                rs = slice(c * rows_d, (c + 1) * rows_d)
                wd_b[rs, :] = stage_d[rs, :].astype(BF16)

            @pl.when(ne_ref[b] != expert)
            def _next_weights():
                for c in weight_copies(ne_ref[b]):
                    c.start()

        for s in range(n_chunks):
            xs[:, s * LANES:(s + 1) * LANES] = _rows_chunk(x_ref, s, blk, n_chunks).astype(xs.dtype)
        wide = 2 * LANES
        for c in range(d_ff // wide):
            a = jnp.dot(xs[...], wgu_b[:, c * wide:(c + 1) * wide], preferred_element_type=F32)
            gate = jnp.dot(xs[...], wgu_b[:, d_ff + c * wide:d_ff + (c + 1) * wide], preferred_element_type=F32)
            act[:, c * wide:(c + 1) * wide] = (_silu(a) * gate).astype(act.dtype)
        for c in range(d // (2 * wide)):
            y = jnp.dot(act[...], wd_b[:, c * 2 * wide:(c + 1) * 2 * wide], preferred_element_type=F32)
            for s in range(4):
                o_ref[pl.ds(4 * c + s, blk, stride=n_chunks), :] = y[:, s * LANES:(s + 1) * LANES]

    @pl.when(jnp.logical_not(live))
    def _idle():
        _zero_rows(o_ref)


def _experts(x_sorted, w_gu, w_down, layer, block_e, n_used, next_e):
    _, n_exp, d, two_ff = w_gu.shape
    d_ff = two_ff // 2
    blk = ROW_TILE
    n_chunks = d // LANES
    n_blocks = block_e.shape[0]
    kern = functools.partial(_expert_kernel, layer=layer, n_chunks=n_chunks)
    return pl.pallas_call(
        kern,
        out_shape=jax.ShapeDtypeStruct((n_blocks * blk * n_chunks, LANES), F32),
        grid_spec=pltpu.PrefetchScalarGridSpec(
            num_scalar_prefetch=3,
            grid=(n_blocks,),
            in_specs=[
                pl.BlockSpec((blk * n_chunks, LANES), lambda i, *_: (i, 0)),
                pl.BlockSpec(memory_space=pl.ANY),
                pl.BlockSpec(memory_space=pl.ANY),
            ],
            out_specs=pl.BlockSpec((blk * n_chunks, LANES), lambda i, *_: (i, 0)),
            scratch_shapes=[
                pltpu.VMEM((blk, d), BF16),
                pltpu.VMEM((blk, d_ff), BF16),
                pltpu.VMEM((d, two_ff), BF16),
                pltpu.VMEM((d_ff, d), BF16),
                pltpu.VMEM((d, two_ff), F32),
                pltpu.VMEM((d_ff, d), F32),
                pltpu.SemaphoreType.DMA((2,)),
            ],
        ),
        compiler_params=_params("arbitrary"),
        name="moe_experts",
    )(block_e, n_used, next_e, x_sorted, w_gu, w_down)


def _combine_kernel(dst0_ref, dstn_ref, ys_hbm, x_ref, gate_ref, g2_ref, ng_ref, nsh_ref, nsc_ref, *refs,
                    n_chunks, final_norm):
    if final_norm:
        o_ref, ybuf, sem = refs
        h_ref = o_ref
    else:
        o_ref, h_ref, ybuf, sem = refs
    i = pl.program_id(0)
    n = pl.num_programs(0)
    blk, d = x_ref.shape
    pitch = VMEM_ROW_PITCH
    slot = lax.rem(i, 2)
    tile_rows = blk * n_chunks

    def gather_start(idx_ref, buf):
        for r in range(blk):
            for k in range(TOP_K):
                src = pl.multiple_of(idx_ref[0, k * blk + r], n_chunks)
                pltpu.make_async_copy(ys_hbm.at[pl.ds(src, n_chunks)],
                                      ybuf.at[buf, k, pl.ds(r * pitch, n_chunks)],
                                      sem.at[buf]).start(priority=k % DMA_PRIORITIES)

    @pl.when(i == 0)
    def _prologue():
        gather_start(dst0_ref, 0)

    @pl.when(i + 1 < n)
    def _prefetch():
        gather_start(dstn_ref, 1 - slot)

    for k in range(TOP_K):
        pltpu.make_async_copy(ys_hbm.at[pl.ds(0, tile_rows)],
                              ybuf.at[slot, k, pl.ds(0, tile_rows)], sem.at[slot]).wait()
    gates = [jnp.broadcast_to(gate_ref[:, k:k + 1], (blk, LANES)) for k in range(TOP_K)]
    sq = jnp.zeros((blk, 1), F32)
    for s in range(n_chunks):
        cs = slice(s * LANES, (s + 1) * LANES)
        f = (_rows_chunk(ybuf.at[slot, 0], s, blk, pitch) * gates[0]
             + _rows_chunk(ybuf.at[slot, 1], s, blk, pitch) * gates[1])
        xo = x_ref[:, cs] + g2_ref[0, :, cs] * f
        o_ref[:, cs] = xo
        sq = sq + jnp.sum(xo * xo, axis=-1, keepdims=True)
    r = lax.rsqrt(sq * (1.0 / d) + NORM_EPS)
    for s in range(n_chunks):
        cs = slice(s * LANES, (s + 1) * LANES)
        hn = (o_ref[:, cs] * r) * ng_ref[:, cs] * (1.0 + nsc_ref[0, :, cs]) + nsh_ref[0, :, cs]
        h_ref[:, cs] = hn.astype(h_ref.dtype)


def _combine(ys, dest_tiles, x_tok, gates, g2, norm_g, norm_shift, norm_scale, n_rows, seq, final_norm):
    assert TOP_K == 2
    d = x_tok.shape[1]
    n_batch = g2.shape[0] - 1
    blk = ROW_TILE
    n_chunks = d // LANES
    tiles = n_rows // blk
    mod = _mod_index(seq // blk, n_batch)
    kern = functools.partial(_combine_kernel, n_chunks=n_chunks, final_norm=final_norm)
    smem_row = lambda index: pl.BlockSpec((None, 1, TOP_K * blk), index, memory_space=pltpu.SMEM)
    row_spec = pl.BlockSpec((blk, d), lambda i: (i, 0))
    if final_norm:
        out_shape, out_specs = jax.ShapeDtypeStruct((n_rows, d), F32), row_spec
    else:
        out_shape = (jax.ShapeDtypeStruct((n_rows, d), F32), jax.ShapeDtypeStruct((n_rows, d), BF16))
        out_specs = (row_spec, row_spec)
    return pl.pallas_call(
        kern,
        out_shape=out_shape,
        grid=(tiles,),
        in_specs=[
            smem_row(lambda i: (0, 0, 0)),
            smem_row(lambda i: (jnp.minimum(i + 1, tiles - 1), 0, 0)),
            pl.BlockSpec(memory_space=pl.ANY),
            pl.BlockSpec((blk, d), lambda i: (i, 0)),
            pl.BlockSpec((blk, TOP_K), lambda i: (i, 0)),
            pl.BlockSpec((1, 1, d), mod),
            pl.BlockSpec((1, d), lambda i: (0, 0)),
            pl.BlockSpec((1, 1, d), mod),
            pl.BlockSpec((1, 1, d), mod),
        ],
        out_specs=out_specs,
        scratch_shapes=[
            pltpu.VMEM((2, TOP_K, blk * VMEM_ROW_PITCH, LANES), F32),
            pltpu.SemaphoreType.DMA((2,)),
        ],
        compiler_params=_params("arbitrary"),
        name="moe_combine",
    )(dest_tiles, dest_tiles, ys, x_tok, gates.T, g2, norm_g.reshape(1, d), norm_shift, norm_scale)


def kernel(x, c, ctx, c_ctx, w_ada, b_ada, norm1_g, norm2_g, w_in, ret_decay, pool_w, pool_scale, sgu_w, sgu_b,
           w_branch, w_out, w_router, router_bias, w_gu, w_down, final_g):
    n_batch, seq, d = x.shape
    ctx_len = ctx.shape[1]
    depth = w_ada.shape[0]
    bw = d // 2
    t_x = n_batch * seq
    t_all = t_x + n_batch * ctx_len
    n_exp = w_router.shape[1]
    assert seq % ROW_TILE == 0 and ctx_len % ROW_TILE == 0 and ctx_len % RET_CHUNK == 0
    assert GRID_W & (GRID_W - 1) == 0 and seq % GRID_W == 0

    pad = (-(n_batch + 1)) % 8
    c_all = jnp.concatenate([c, c_ctx[None, :], jnp.zeros((pad, d), F32)], axis=0)
    ada = _ada(c_all, w_ada, b_ada)
    mods = [[m[:n_batch + 1, None, :] for m in jnp.split(ada[l], 6, axis=-1)] for l in range(depth)]
    no_mod = jnp.zeros((n_batch + 1, 1, d), F32)
    cos_t, sin_t = _rope_tables(n_batch, seq, ctx_len)
    w_branch_b, w_out_b = w_branch.astype(BF16), w_out.astype(BF16)

    x_main, x_ctx = x.reshape(t_x, d), ctx.reshape(n_batch * ctx_len, d)
    h = _norm(x_main, x_ctx, norm1_g[0], mods[0][0], mods[0][1], seq)
    out = None
    for l in range(depth):
        last = l == depth - 1
        sh1, sc1, g1, sh2, sc2, g2 = mods[l]
        rows_mix = t_x if last else t_all
        p = _in_proj(h, w_in, l, cos_t, sin_t, t_all, tm=_lane_tile(2304, t_all), tn=bw)
        lg = jax.nn.log_sigmoid(ret_decay[l].astype(F32)).T
        lg = jnp.broadcast_to(lg[:, :, None], lg.shape + (RET_CHUNK,))
        ret = _retention(p, lg, n_batch, seq, ctx_len)
        pooled = _pool(p, pool_w[l].astype(BF16), pool_scale[l], 4 * bw, n_batch, seq, ctx_len, not last)
        gated = _sgu(p, sgu_w[l], sgu_b[l], 5 * bw, bw, rows_mix)
        merged = _merge((ret, pooled, gated), w_branch_b, l, p, 7 * bw, rows_mix)
        x_tok, eid, gates = _matmul_residual(merged, w_out_b, l, x_main, x_ctx if rows_mix > t_x else None, g1,
                                             norm2_g[l], sh2, sc2, w_router, router_bias, rows_mix, seq)
        dest_tiles, block_e, n_used, next_e, pad_start, pad_cnt = _route_plan(eid, n_exp, d // LANES)
        x_sorted = _dispatch(x_tok, norm2_g[l], sh2, sc2, dest_tiles, pad_start, pad_cnt, n_used,
                             rows_mix, block_e.shape[0] * ROW_TILE, seq)
        ys = _experts(x_sorted, w_gu, w_down, l, block_e, n_used, next_e)
        if last:
            out = _combine(ys, dest_tiles, x_tok, gates, g2, final_g, no_mod, no_mod, rows_mix, seq, True)
        else:
            x_main, h = _combine(ys, dest_tiles, x_tok, gates, g2, norm1_g[l + 1], mods[l + 1][0], mods[l + 1][1],
                                 rows_mix, seq, False)
            x_ctx = x_main[t_x:]
    return out.reshape(n_batch, seq, d)
```

```python
import functools

import numpy as np
import jax
import jax.numpy as jnp
from jax import lax
from jax.experimental import pallas as pl
from jax.experimental.pallas import tpu as pltpu

F32 = jnp.float32
BF16 = jnp.bfloat16

GRID_W = 64
RET_DK = 128
POOL_WINDOWS = (2, 4, 8, 16)
SGU_CHUNK = 128
N_EXPERT_GROUPS = 4
TOP_K = 2
ROPE_BASE = 10000.0
NORM_EPS = 1e-6

V7X_VMEM_LIMIT_BYTES = 60 * 1024 * 1024
LANES = 128

ROW_TILE = 256
RET_CHUNK = 256
POOL_HALO = 512


def _tile(pref, *sizes):
    t = pref
    while any(s % t for s in sizes):
        t //= 2
    return t


def _lane_tile(cap, size):
    return max(t for t in range(LANES, cap + 1, LANES) if size % t == 0)


def _params(*sem):
    return pltpu.CompilerParams(dimension_semantics=sem, vmem_limit_bytes=V7X_VMEM_LIMIT_BYTES)


def _silu(x):
    return x * jax.nn.sigmoid(x)


def _ada_kernel(c_ref, w_ref, b_ref, o_ref):
    a = _silu(c_ref[...]).astype(BF16)
    o_ref[...] = jnp.dot(a, w_ref[...].astype(BF16), preferred_element_type=F32) + b_ref[...]


def _ada(c_all, w_ada, b_ada, tn=1024):
    depth, d, n = w_ada.shape
    rows = c_all.shape[0]
    return pl.pallas_call(
        _ada_kernel,
        out_shape=jax.ShapeDtypeStruct((depth, rows, n), F32),
        grid=(depth, n // tn),
        in_specs=[
            pl.BlockSpec((rows, d), lambda l, j: (0, 0)),
            pl.BlockSpec((None, d, tn), lambda l, j: (l, 0, j)),
            pl.BlockSpec((None, 1, tn), lambda l, j: (l, 0, j)),
        ],
        out_specs=pl.BlockSpec((None, rows, tn), lambda l, j: (l, 0, j)),
        compiler_params=_params("arbitrary", "arbitrary"),
        name="ada_proj",
    )(c_all, w_ada, b_ada.reshape(depth, 1, n))


def _modulate(x, g, shift, scale):
    r = lax.rsqrt(jnp.mean(x * x, axis=-1, keepdims=True) + NORM_EPS)
    return (x * r) * g * (1.0 + scale) + shift


def _two_source_specs(block, main_tiles, grid_rank):
    main = lambda i: jnp.minimum(i, main_tiles - 1)
    ctx = lambda i: jnp.maximum(i - main_tiles, 0)
    if grid_rank == 1:
        return pl.BlockSpec(block, lambda i: (main(i), 0)), pl.BlockSpec(block, lambda i: (ctx(i), 0))
    return pl.BlockSpec(block, lambda i, j: (main(i), j)), pl.BlockSpec(block, lambda i, j: (ctx(i), j))


def _two_source_pick(i, main_tiles, main_ref, ctx_ref):
    return jnp.where(i < main_tiles, main_ref[...], ctx_ref[...])


def _norm_kernel(xm_ref, xc_ref, g_ref, sh_ref, sc_ref, o_ref, *, main_tiles):
    x = _two_source_pick(pl.program_id(0), main_tiles, xm_ref, xc_ref)
    o_ref[...] = _modulate(x, g_ref[...], sh_ref[0], sc_ref[0]).astype(o_ref.dtype)


def _mod_index(tiles_per_batch, n_batch):
    return lambda i: (jnp.minimum(i // tiles_per_batch, n_batch), 0, 0)


def _norm(x_main, x_ctx, g, shift, scale, seq):
    d = x_main.shape[1]
    n_rows = x_main.shape[0] + x_ctx.shape[0]
    n_batch = shift.shape[0] - 1
    main_tiles = x_main.shape[0] // ROW_TILE
    mod = _mod_index(seq // ROW_TILE, n_batch)
    xm_spec, xc_spec = _two_source_specs((ROW_TILE, d), main_tiles, 1)
    return pl.pallas_call(
        functools.partial(_norm_kernel, main_tiles=main_tiles),
        out_shape=jax.ShapeDtypeStruct((n_rows, d), BF16),
        grid=(n_rows // ROW_TILE,),
        in_specs=[
            xm_spec,
            xc_spec,
            pl.BlockSpec((1, d), lambda i: (0, 0)),
            pl.BlockSpec((1, 1, d), mod),
            pl.BlockSpec((1, 1, d), mod),
        ],
        out_specs=pl.BlockSpec((ROW_TILE, d), lambda i: (i, 0)),
        compiler_params=_params("arbitrary"),
        name="mod_norm",
    )(x_main, x_ctx, g.reshape(1, d), shift, scale)


def _in_proj_kernel(a_ref, b_ref, cos_ref, sin_ref, o_ref, wb, *, rope_tiles, qscale):
    j = pl.program_id(0)

    @pl.when(pl.program_id(1) == 0)
    def _cast():
        rows = wb.shape[0] // 8
        for c in range(8):
            wb[c * rows:(c + 1) * rows, :] = b_ref[c * rows:(c + 1) * rows, :].astype(wb.dtype)

    @pl.when(j >= rope_tiles)
    def _plain():
        wide = o_ref.shape[1] // 2
        for c in range(2):
            cs = slice(c * wide, (c + 1) * wide)
            o_ref[:, cs] = jnp.dot(a_ref[...], wb[:, cs], preferred_element_type=F32).astype(o_ref.dtype)

    @pl.when(j < rope_tiles)
    def _rotated():
        scale = jnp.where(j == 0, qscale, 1.0)
        pair = 2 * RET_DK
        half_rows = o_ref.shape[0] // 2
        for r in range(2):
            rs = slice(r * half_rows, (r + 1) * half_rows)
            cosv = cos_ref[rs, :]
            sinv = sin_ref[rs, :]
            for c in range(o_ref.shape[1] // pair):
                t2 = jnp.dot(a_ref[rs, :], wb[:, c * pair:(c + 1) * pair], preferred_element_type=F32)
                for hh in range(2):
                    t = t2[:, hh * RET_DK:(hh + 1) * RET_DK]
                    hs = slice(c * pair + hh * RET_DK, c * pair + (hh + 1) * RET_DK)
                    rotated = t * cosv + pltpu.roll(t, RET_DK // 2, 1) * sinv
                    o_ref[rs, hs] = (rotated * scale).astype(o_ref.dtype)


def _in_proj(a, b, layer, cos_t, sin_t, n_rows, tm, tn):
    k = a.shape[1]
    n = b.shape[2]
    rope_tiles = 2
    table = lambda j, i: (jnp.where(j < rope_tiles, i, 0), 0)
    kern = functools.partial(_in_proj_kernel, rope_tiles=rope_tiles, qscale=float(RET_DK) ** -0.5)
    return pl.pallas_call(
        kern,
        out_shape=jax.ShapeDtypeStruct((n_rows, n), BF16),
        grid=(n // tn, n_rows // tm),
        in_specs=[
            pl.BlockSpec((tm, k), lambda j, i: (i, 0)),
            pl.BlockSpec((None, k, tn), lambda j, i: (layer, 0, j)),
            pl.BlockSpec((tm, RET_DK), table),
            pl.BlockSpec((tm, RET_DK), table),
        ],
        out_specs=pl.BlockSpec((tm, tn), lambda j, i: (i, j)),
        scratch_shapes=[pltpu.VMEM((k, tn), a.dtype)],
        compiler_params=_params("arbitrary", "arbitrary"),
        name="in_proj",
    )(a, b, cos_t, sin_t)


def _mm_residual_kernel(a_ref, b_ref, g_ref, ng_ref, nsh_ref, nsc_ref, wr_ref, br_ref, *refs, main_tiles):
    *x_refs, o_ref, eid_ref, gate_ref = refs
    y = g_ref[0] * jnp.dot(a_ref[...], b_ref[...], preferred_element_type=F32)
    if len(x_refs) == 1:
        o_ref[...] = x_refs[0][...] + y
    else:
        is_main = pl.program_id(0) < main_tiles

        @pl.when(is_main)
        def _main():
            o_ref[...] = x_refs[0][...] + y

        @pl.when(jnp.logical_not(is_main))
        def _ctx():
            o_ref[...] = x_refs[1][...] + y

    h = _modulate(o_ref[...], ng_ref[...], nsh_ref[0], nsc_ref[0]).astype(BF16)
    _route_tokens(h, wr_ref, br_ref, eid_ref, gate_ref)


def _matmul_residual(a, b, layer, x_main, x_ctx, gate, norm_g, norm_shift, norm_scale, w_router, router_bias,
                     n_rows, seq):
    k = a.shape[1]
    n = b.shape[2]
    n_exp = w_router.shape[1]
    n_batch = gate.shape[0] - 1
    tm = _tile(512, n_rows, seq) if x_ctx is None else _tile(512, n_rows, seq, x_ctx.shape[0])
    tiles_per_batch = seq // tm
    main_tiles = n_batch * tiles_per_batch
    mod = _mod_index(tiles_per_batch, n_batch)
    if x_ctx is None:
        x_specs, x_args = [pl.BlockSpec((tm, n), lambda i: (i, 0))], [x_main]
    else:
        x_specs, x_args = list(_two_source_specs((tm, n), main_tiles, 1)), [x_main, x_ctx]
    return pl.pallas_call(
        functools.partial(_mm_residual_kernel, main_tiles=main_tiles),
        out_shape=(
            jax.ShapeDtypeStruct((n_rows, n), F32),
            jax.ShapeDtypeStruct((TOP_K, n_rows), jnp.int32),
            jax.ShapeDtypeStruct((TOP_K, n_rows), F32),
        ),
        grid=(n_rows // tm,),
        in_specs=[
            pl.BlockSpec((tm, k), lambda i: (i, 0)),
            pl.BlockSpec((None, k, n), lambda i: (layer, 0, 0)),
            pl.BlockSpec((1, 1, n), mod),
            pl.BlockSpec((1, n), lambda i: (0, 0)),
            pl.BlockSpec((1, 1, n), mod),
            pl.BlockSpec((1, 1, n), mod),
            pl.BlockSpec((n_exp, n), lambda i: (0, 0)),
            pl.BlockSpec((n_exp, 1), lambda i: (0, 0)),
        ] + x_specs,
        out_specs=(
            pl.BlockSpec((tm, n), lambda i: (i, 0)),
            pl.BlockSpec((TOP_K, tm), lambda i: (0, i)),
            pl.BlockSpec((TOP_K, tm), lambda i: (0, i)),
        ),
        compiler_params=_params("arbitrary"),
        name="out_proj_residual",
    )(a, b, gate, norm_g.reshape(1, n), norm_shift, norm_scale, w_router.T.astype(BF16),
      router_bias.reshape(n_exp, 1), *x_args)


def _spatial_gate(u_ref, v_ref, w_ref, b_ref, o_ref):
    n_groups = w_ref.shape[0]
    gw = o_ref.shape[1] // n_groups
    v = v_ref[...].astype(F32)
    mu = jnp.mean(v, axis=-1, keepdims=True)
    vc = v - mu
    var = jnp.mean(vc * vc, axis=-1, keepdims=True)
    vn = (vc * lax.rsqrt(var + NORM_EPS)).astype(BF16)
    for kc in range(o_ref.shape[0] // SGU_CHUNK):
        rs = slice(kc * SGU_CHUNK, (kc + 1) * SGU_CHUNK)
        for g in range(n_groups):
            cs = slice(g * gw, (g + 1) * gw)
            mixed = jnp.dot(w_ref[g], vn[rs, cs], preferred_element_type=F32) + b_ref[:, g:g + 1]
            o_ref[rs, cs] = (u_ref[rs, cs].astype(F32) * mixed).astype(o_ref.dtype)


def _merge_kernel(*refs, n_branch, gate_parts):
    n_in = n_branch - 1
    u_ref, v_ref, ws_ref, bs_ref = refs[n_in:n_in + 4]
    b_refs = refs[n_in + 4:n_in + 4 + n_branch]
    g_refs = refs[n_in + 4 + n_branch:-2]
    o_ref, sgu_out = refs[-2:]
    _spatial_gate(u_ref, v_ref, ws_ref, bs_ref, sgu_out)
    a_refs = tuple(refs[:n_in]) + (sgu_out,)
    wide = 4 * LANES
    part = o_ref.shape[1] // gate_parts
    for c in range(o_ref.shape[1] // wide):
        cs = slice(c * wide, (c + 1) * wide)
        gs = slice(c * wide % part, c * wide % part + wide)
        acc = None
        for s in range(n_branch):
            g_ref = g_refs[s * gate_parts + c * wide // part]
            y = (jax.nn.sigmoid(g_ref[:, gs].astype(F32))
                 * jnp.dot(a_refs[s][...], b_refs[s][:, cs], preferred_element_type=F32))
            acc = y if acc is None else acc + y
        o_ref[:, cs] = acc.astype(o_ref.dtype)


def _merge(branches, sgu_w, sgu_b, w_branch, layer, p, sgu_col0, gate_col0, n_rows):
    tm = _tile(512, n_rows)
    n_branch = len(branches) + 1
    kb = branches[0].shape[1]
    n = w_branch.shape[3]
    gate_parts = 2
    gw = n // gate_parts
    assert gate_col0 % gw == 0 and sgu_col0 % kb == 0
    n_groups = sgu_w.shape[0]
    a_specs = [pl.BlockSpec((tm, kb), lambda i: (i, 0)) for _ in branches]
    a_specs += [
        pl.BlockSpec((tm, kb), lambda i: (i, sgu_col0 // kb)),
        pl.BlockSpec((tm, kb), lambda i: (i, sgu_col0 // kb + 1)),
        pl.BlockSpec(sgu_w.shape, lambda i: (0, 0, 0)),
        pl.BlockSpec((SGU_CHUNK, n_groups), lambda i: (0, 0)),
    ]
    b_specs = [pl.BlockSpec((None, None, kb, n), functools.partial(lambda i, s: (layer, s, 0, 0), s=s),
                            pipeline_mode=pl.Buffered(1))
               for s in range(n_branch)]
    g_specs = [
        pl.BlockSpec((tm, gw), functools.partial(lambda i, c: (i, c), c=(gate_col0 + s * n) // gw + part))
        for s in range(n_branch) for part in range(gate_parts)
    ]
    return pl.pallas_call(
        functools.partial(_merge_kernel, n_branch=n_branch, gate_parts=gate_parts),
        out_shape=jax.ShapeDtypeStruct((n_rows, n), BF16),
        grid=(n_rows // tm,),
        in_specs=a_specs + b_specs + g_specs,
        out_specs=pl.BlockSpec((tm, n), lambda i: (i, 0)),
        scratch_shapes=[pltpu.VMEM((tm, kb), BF16)],
        compiler_params=_params("arbitrary"),
        name="branch_merge",
    )(*branches, p, p, sgu_w.astype(BF16), sgu_b.T, *([w_branch] * n_branch), *([p] * (n_branch * gate_parts)))


def _retention_kernel(q_ref, k_ref, v_ref, g_ref, lg_ref, o_ref,
                      sf_all, s_run, dmat, rw, *, n_heads, nc_ctx, nc_x):
    pss = pl.program_id(1)
    j = pl.program_id(2)
    c = RET_CHUNK
    dk = RET_DK

    @pl.when(jnp.logical_and(pss == 0, j == 0))
    def _tables():
        row = lax.broadcasted_iota(jnp.int32, (c, c), 0)
        col = lax.broadcasted_iota(jnp.int32, (c, c), 1)
        diff = (row - col).astype(F32)
        pos = lax.broadcasted_iota(jnp.int32, (c, dk), 0).astype(F32)
        for h in range(n_heads):
            lgf = lg_ref[h, 0:1, :]
            lgb = lg_ref[h, 1:2, :]
            fwd = jnp.where(diff >= 0, jnp.exp(jnp.maximum(diff, 0.0) * lgf), 0.0)
            bwd = jnp.where(diff <= 0, jnp.exp(jnp.maximum(-diff, 0.0) * lgb), 0.0)
            dmat[h] = fwd + bwd
            lgf1 = lgf[:, :dk]
            lgb1 = lgb[:, :dk]
            rw[h, 0] = jnp.exp((pos + 1.0) * lgf1)
            rw[h, 1] = jnp.exp((c - pos) * lgb1)
            rw[h, 2] = jnp.exp((c - 1.0 - pos) * lgf1)
            rw[h, 3] = jnp.exp(pos * lgb1)

    @pl.when(j == 0)
    def _reset():
        s_run[...] = jnp.zeros_like(s_run)

    def state_update(h, kk, vv, direction):
        wv = (vv.astype(F32) * rw[h, 2 + direction]).astype(BF16)
        kv = lax.dot_general(kk, wv, (((0,), (0,)), ((), ())), preferred_element_type=F32)
        decay = jnp.exp(float(c) * lg_ref[h, direction:direction + 1, 0:dk])
        s_run[h] = decay * s_run[h] + kv

    @pl.when(pss == 0)
    def _forward():
        for h in range(n_heads):
            cs = slice(h * dk, (h + 1) * dk)
            sf_all[j, h] = s_run[h]
            state_update(h, k_ref[:, cs], v_ref[:, cs], 0)

    @pl.when(pss == 1)
    def _backward():
        idx = jnp.where(j < nc_ctx, nc_ctx - 1 - j, nc_ctx + nc_x - 1 - (j - nc_ctx))
        for h in range(n_heads):
            cs = slice(h * dk, (h + 1) * dk)
            kk = k_ref[:, cs]
            qq = q_ref[:, cs]
            vv = v_ref[:, cs]
            scores = lax.dot_general(qq, kk, (((1,), (1,)), ((), ())), preferred_element_type=F32)
            inner = jnp.dot((scores * dmat[h]).astype(BF16), vv, preferred_element_type=F32)
            states = jnp.concatenate([sf_all[idx, h], s_run[h]], axis=1).astype(BF16)
            cross = jnp.dot(qq, states, preferred_element_type=F32)
            y = inner + rw[h, 0] * cross[:, :dk] + rw[h, 1] * cross[:, dk:]
            mu = jnp.mean(y, axis=-1, keepdims=True)
            yc = y - mu
            var = jnp.mean(yc * yc, axis=-1, keepdims=True)
            yn = yc * lax.rsqrt(var + NORM_EPS)
            o_ref[:, cs] = (yn * _silu(g_ref[:, cs].astype(F32))).astype(o_ref.dtype)
            state_update(h, kk, vv, 1)


def _retention(p, lg, n_batch, seq, ctx_len):
    bw = lg.shape[0] * RET_DK
    n_heads = lg.shape[0]
    c = RET_CHUNK
    nc_x = seq // c
    nc_ctx = ctx_len // c
    nc = nc_x + nc_ctx
    t = n_batch * (seq + ctx_len)

    def fwd_idx(pss, j):
        bwd = jnp.where(j < nc_ctx, nc_ctx - 1 - j, nc_ctx + nc_x - 1 - (j - nc_ctx))
        return jnp.where(pss == 0, j, bwd)

    def row_block(b, idx):
        return jnp.where(idx < nc_ctx, n_batch * nc_x + b * nc_ctx + idx, b * nc_x + (idx - nc_ctx))

    def kv_map(col):
        return lambda b, pss, j: (row_block(b, fwd_idx(pss, j)), col)

    def qg_map(col):
        return lambda b, pss, j: (row_block(b, fwd_idx(1, jnp.where(pss == 0, 0, j))), col)

    kern = functools.partial(_retention_kernel, n_heads=n_heads, nc_ctx=nc_ctx, nc_x=nc_x)
    return pl.pallas_call(
        kern,
        out_shape=jax.ShapeDtypeStruct((t, bw), BF16),
        grid=(n_batch, 2, nc),
        in_specs=[
            pl.BlockSpec((c, bw), qg_map(0)),
            pl.BlockSpec((c, bw), kv_map(1)),
            pl.BlockSpec((c, bw), kv_map(2)),
            pl.BlockSpec((c, bw), qg_map(3)),
            pl.BlockSpec((n_heads, 2, c), lambda b, pss, j: (0, 0, 0)),
        ],
        out_specs=pl.BlockSpec((c, bw), qg_map(0)),
        scratch_shapes=[
            pltpu.VMEM((nc, n_heads, RET_DK, RET_DK), F32),
            pltpu.VMEM((n_heads, RET_DK, RET_DK), F32),
            pltpu.VMEM((n_heads, c, c), F32),
            pltpu.VMEM((n_heads, 4, c, RET_DK), F32),
        ],
        compiler_params=_params("arbitrary", "arbitrary", "arbitrary"),
        name="retention",
    )(p, p, p, p, lg)


def _rope_tables(n_batch, seq, ctx_len):
    t = np.arange(seq)
    rows = (t // GRID_W).astype(np.float32)
    cols = (t % GRID_W).astype(np.float32)
    quarter = RET_DK // 4
    inv = jnp.asarray(ROPE_BASE, F32) ** (-jnp.arange(quarter, dtype=F32) / quarter)
    ang = jnp.concatenate([jnp.asarray(rows)[:, None] * inv, jnp.asarray(cols)[:, None] * inv], axis=-1)
    cos, sin = jnp.cos(ang), jnp.sin(ang)
    cos_x = jnp.tile(jnp.concatenate([cos, cos], axis=-1), (n_batch, 1))
    sin_x = jnp.tile(jnp.concatenate([-sin, sin], axis=-1), (n_batch, 1))
    cos_t = jnp.concatenate([cos_x, jnp.ones((n_batch * ctx_len, RET_DK), F32)], axis=0)
    sin_t = jnp.concatenate([sin_x, jnp.zeros((n_batch * ctx_len, RET_DK), F32)], axis=0)
    return cos_t, sin_t


def _pool_kernel(x_ref, win_ref, w_ref, sc_ref, o_ref, xpad, *, n_tok, halo, grid_rows):
    group = pl.program_id(1)
    slab = ROW_TILE
    if halo:
        xpad[0:halo, :] = jnp.zeros((halo, xpad.shape[1]), xpad.dtype)
        xpad[halo + n_tok:, :] = jnp.zeros((halo, xpad.shape[1]), xpad.dtype)
    xpad[halo:halo + n_tok, :] = x_ref[...]

    def mix(half, lo, width):
        for s in range(n_tok // slab):
            total = jnp.dot(win_ref[:, lo:lo + width], xpad[s * slab + lo:s * slab + lo + width, :],
                            preferred_element_type=F32)
            tok = s * slab + lax.broadcasted_iota(jnp.int32, total.shape, 0)
            if grid_rows is None:
                count = jnp.minimum(tok + half, n_tok) - jnp.maximum(tok - half, 0)
            else:
                col = jnp.bitwise_and(tok, GRID_W - 1)
                row = jnp.right_shift(tok, GRID_W.bit_length() - 1)
                count = ((jnp.minimum(col + half, GRID_W) - jnp.maximum(col - half, 0))
                         * (jnp.minimum(row + half, grid_rows) - jnp.maximum(row - half, 0)))
            mean = total * (1.0 / count.astype(F32))
            d = (mean - x_ref[s * slab:(s + 1) * slab, :].astype(F32)).astype(BF16)
            y = jnp.dot(d, w_ref[...], preferred_element_type=F32) * sc_ref[...]
            o_ref[s * slab:(s + 1) * slab, :] = y.astype(o_ref.dtype)

    if grid_rows is None:
        mix(jnp.left_shift(1, group), 0, win_ref.shape[1])
    else:
        for k, window in enumerate(POOL_WINDOWS):
            reach = window // 2 * GRID_W
            lo = (halo - reach) // LANES * LANES
            hi = -(-(halo + slab + reach) // LANES) * LANES

            @pl.when(group == k)
            def _group(window=window, lo=lo, hi=hi):
                mix(window // 2, lo, hi - lo)


def _pool_windows_2d():
    a = np.arange(ROW_TILE)
    s = np.arange(ROW_TILE + 2 * POOL_HALO)
    ra, ca = a // GRID_W, a % GRID_W
    rs, cs = s // GRID_W - POOL_HALO // GRID_W, s % GRID_W
    out = []
    for w in POOL_WINDOWS:
        h = w // 2
        dr = rs[None, :] - ra[:, None]
        dc = cs[None, :] - ca[:, None]
        out.append(((dr >= -h) & (dr <= h - 1) & (dc >= -h) & (dc <= h - 1)).astype(np.float32))
    return jnp.asarray(np.stack(out), BF16)


def _pool_windows_1d(n):
    a = np.arange(n)
    out = []
    for w in POOL_WINDOWS:
        h = w // 2
        ds = a[None, :] - a[:, None]
        out.append(((ds >= -h) & (ds <= h - 1)).astype(np.float32))
    return jnp.asarray(np.stack(out), BF16)


def _pool(p, pool_w, pool_scale, col0, n_batch, seq, ctx_len, with_ctx):
    n_groups, gw, _ = pool_w.shape
    bw = n_groups * gw
    assert POOL_HALO >= max(POOL_WINDOWS) // 2 * GRID_W and gw == ROW_TILE
    scale = pool_scale.reshape(1, bw)
    cb = col0 // gw

    def call(n_tok, halo, grid_rows, row0_blocks, windows):
        kern = functools.partial(_pool_kernel, n_tok=n_tok, halo=halo, grid_rows=grid_rows)
        return pl.pallas_call(
            kern,
            out_shape=jax.ShapeDtypeStruct((n_batch * n_tok, bw), BF16),
            grid=(n_batch, n_groups),
            in_specs=[
                pl.BlockSpec((n_tok, gw), lambda b, g: (row0_blocks + b, cb + g)),
                pl.BlockSpec((None,) + windows.shape[1:], lambda b, g: (g, 0, 0)),
                pl.BlockSpec((None, gw, gw), lambda b, g: (g, 0, 0)),
                pl.BlockSpec((1, gw), lambda b, g: (0, g)),
            ],
            out_specs=pl.BlockSpec((n_tok, gw), lambda b, g: (b, g)),
            scratch_shapes=[pltpu.VMEM((n_tok + 2 * halo, gw), BF16)],
            compiler_params=_params("arbitrary", "arbitrary"),
            name="pool_mix_ctx" if grid_rows is None else "pool_mix",
        )(p, windows, pool_w, scale)

    out = call(seq, POOL_HALO, seq // GRID_W, 0, _pool_windows_2d())
    if with_ctx:
        out_ctx = call(ctx_len, 0, None, n_batch * seq // ctx_len, _pool_windows_1d(ctx_len))
        out = jnp.concatenate([out, out_ctx], axis=0)
    return out


VMEM_ROW_PITCH = 24
DMA_PRIORITIES = 2


def _rows_store(dst_ref, v, pitch):
    rows = v.shape[0]
    for s in range(v.shape[1] // LANES):
        dst_ref[pl.ds(s, rows, stride=pitch), :] = v[:, s * LANES:(s + 1) * LANES].astype(dst_ref.dtype)


def _zero_rows(ref, piece=512):
    for c in range(ref.shape[0] // piece):
        ref[c * piece:(c + 1) * piece, :] = jnp.zeros((piece, ref.shape[1]), ref.dtype)


def _rows_chunk(src_ref, s, rows, pitch):
    return src_ref[pl.ds(s, rows, stride=pitch), :]


def _route_tokens(h, wr_ref, br_ref, eid_ref, gate_ref):
    n_exp = wr_ref.shape[0]
    logits = lax.dot_general(wr_ref[...], h, (((1,), (1,)), ((), ())), preferred_element_type=F32) + br_ref[...]
    e = jnp.exp(logits - jnp.max(logits, axis=0, keepdims=True))
    probs = e / jnp.sum(e, axis=0, keepdims=True)
    per = n_exp // N_EXPERT_GROUPS
    n_tok = probs.shape[1]
    lane_iota = lax.broadcasted_iota(jnp.int32, (per, n_tok), 0)
    best = None
    for grp in range(N_EXPERT_GROUPS):
        pg = probs[grp * per:(grp + 1) * per, :]
        m1 = jnp.max(pg, axis=0, keepdims=True)
        i1 = jnp.min(jnp.where(pg == m1, lane_iota, per), axis=0, keepdims=True)
        rest = jnp.where(lane_iota == i1, -1.0, pg)
        m2 = jnp.max(rest, axis=0, keepdims=True)
        i2 = jnp.min(jnp.where(rest == m2, lane_iota, per), axis=0, keepdims=True)
        cand = (m1 + m2, m1, m2, i1 + grp * per, i2 + grp * per)
        if best is None:
            best = cand
        else:
            take = cand[0] > best[0]
            best = tuple(jnp.where(take, cn, bs) for cn, bs in zip(cand, best))
    _, m1, m2, e1, e2 = best
    denom = m1 + m2
    eid_ref[0:1, :] = e1
    eid_ref[1:2, :] = e2
    gate_ref[0:1, :] = m1 / denom
    gate_ref[1:2, :] = m2 / denom


def _route_plan(eid, n_exp, n_chunks):
    n_tok = eid.shape[1]
    blk = ROW_TILE
    n_assign = TOP_K * n_tok
    tiles = n_tok // blk
    flat_e = eid.reshape(-1)
    onehot = flat_e[:, None] == jnp.arange(n_exp, dtype=jnp.int32)[None, :]
    oh = onehot.astype(BF16).reshape(n_assign // blk, blk, n_exp)
    tril = jnp.tril(jnp.ones((blk, blk), BF16))
    within = jnp.einsum("ij,cjk->cik", tril, oh, preferred_element_type=F32)
    totals = within[:, -1, :]
    offsets = jnp.cumsum(totals, axis=0) - totals
    csum = (within + offsets[:, None, :]).reshape(n_assign, n_exp).astype(jnp.int32)
    counts = csum[-1]
    rank = jnp.sum(jnp.where(onehot, csum, 0), axis=1) - 1
    padded = ((counts + blk - 1) // blk) * blk
    ends = jnp.cumsum(padded)
    start = ends - padded
    dest = (jnp.sum(jnp.where(onehot, start[None, :], 0), axis=1) + rank).astype(jnp.int32)
    dest_tiles = (dest * n_chunks).reshape(TOP_K, tiles, blk).transpose(1, 0, 2).reshape(tiles, 1, TOP_K * blk)
    n_blocks = -(-(n_assign + n_exp * (blk - 1)) // blk)
    block_start = jnp.arange(n_blocks, dtype=jnp.int32) * blk
    block_e = jnp.minimum(jnp.sum((block_start[:, None] >= ends[None, :]).astype(jnp.int32), axis=1), n_exp - 1)
    n_used = ends[-1] // blk
    live = jnp.arange(n_blocks) < n_used
    later = jnp.logical_and(live[None, :], block_e[None, :] > block_e[:, None])
    next_e = jnp.min(jnp.where(later, block_e[None, :], n_exp), axis=1)
    next_e = jnp.where(next_e == n_exp, block_e, next_e)
    i32 = lambda v: v.astype(jnp.int32)
    pad_start = (start + counts) * n_chunks
    pad_cnt = padded - counts
    return dest_tiles, i32(block_e), i32(n_used).reshape(1), i32(next_e), i32(pad_start), i32(pad_cnt)


def _dispatch_kernel(ps_ref, pc_ref, nu_ref, dst_ref, x_ref, g_ref, sh_ref, sc_ref, out_hbm,
                     stage, zbuf, sem, zsem, *, n_chunks):
    i = pl.program_id(0)
    n = pl.num_programs(0)
    blk = x_ref.shape[0]
    pitch = VMEM_ROW_PITCH
    slot = lax.rem(i, 2)
    tile_rows = blk * n_chunks
    n_blocks = out_hbm.shape[0] // tile_rows

    def scatter_wait(buf):
        for _ in range(TOP_K):
            pltpu.make_async_copy(stage.at[buf, pl.ds(0, tile_rows)],
                                  out_hbm.at[pl.ds(0, tile_rows)], sem.at[buf]).wait()

    @pl.when(i == 0)
    def _fill_padding():
        _zero_rows(zbuf)

        def pad_row(e, r):
            dst = pl.multiple_of(ps_ref[e] + r * n_chunks, n_chunks)
            return pltpu.make_async_copy(zbuf.at[pl.ds(0, n_chunks)], out_hbm.at[pl.ds(dst, n_chunks)], zsem)

        def idle_block(b):
            dst = pl.multiple_of(b * tile_rows, tile_rows)
            return pltpu.make_async_copy(zbuf, out_hbm.at[pl.ds(dst, tile_rows)], zsem)

        def for_all(op):
            def per_expert(e, carry):
                def per_row(r, c):
                    op(pad_row(e, r))
                    return c
                return lax.fori_loop(0, pc_ref[e], per_row, carry)
            lax.fori_loop(0, ps_ref.shape[0], per_expert, 0)

            def per_block(b, c):
                op(idle_block(b))
                return c
            lax.fori_loop(nu_ref[0], n_blocks, per_block, 0)

        for_all(lambda c: c.start())
        for_all(lambda c: c.wait())

    @pl.when(i >= 2)
    def _reuse_stage():
        scatter_wait(slot)

    _rows_store(stage.at[slot], _modulate(x_ref[...], g_ref[...], sh_ref[0], sc_ref[0]), pitch)
    for r in range(blk):
        for k in range(TOP_K):
            dst = pl.multiple_of(dst_ref[0, k * blk + r], n_chunks)
            pltpu.make_async_copy(stage.at[slot, pl.ds(r * pitch, n_chunks)],
                                  out_hbm.at[pl.ds(dst, n_chunks)], sem.at[slot]).start(priority=k % DMA_PRIORITIES)

    @pl.when(i == n - 1)
    def _drain():
        scatter_wait(slot)

        @pl.when(i >= 1)
        def _previous():
            scatter_wait(1 - slot)


def _dispatch(x_tok, g, shift, scale, dest_tiles, pad_start, pad_cnt, n_used, n_rows, n_sorted, seq):
    d = x_tok.shape[1]
    n_chunks = d // LANES
    n_batch = shift.shape[0] - 1
    blk = ROW_TILE
    mod = _mod_index(seq // blk, n_batch)
    kern = functools.partial(_dispatch_kernel, n_chunks=n_chunks)
    return pl.pallas_call(
        kern,
        out_shape=jax.ShapeDtypeStruct((n_sorted * n_chunks, LANES), F32),
        grid_spec=pltpu.PrefetchScalarGridSpec(
            num_scalar_prefetch=3,
            grid=(n_rows // blk,),
            in_specs=[
                pl.BlockSpec((None, 1, TOP_K * blk), lambda i, *_: (i, 0, 0), memory_space=pltpu.SMEM),
                pl.BlockSpec((blk, d), lambda i, *_: (i, 0)),
                pl.BlockSpec((1, d), lambda i, *_: (0, 0)),
                pl.BlockSpec((1, 1, d), lambda i, *_: mod(i)),
                pl.BlockSpec((1, 1, d), lambda i, *_: mod(i)),
            ],
            out_specs=pl.BlockSpec(memory_space=pl.ANY),
            scratch_shapes=[
                pltpu.VMEM((2, blk * VMEM_ROW_PITCH, LANES), F32),
                pltpu.VMEM((blk * n_chunks, LANES), F32),
                pltpu.SemaphoreType.DMA((2,)),
                pltpu.SemaphoreType.DMA,
            ],
        ),
        compiler_params=_params("arbitrary"),
        name="moe_dispatch",
    )(pad_start, pad_cnt, n_used, dest_tiles, x_tok, g.reshape(1, d), shift, scale)


def _expert_kernel(be_ref, nu_ref, ne_ref, x_ref, wgu_hbm, wd_hbm, o_ref,
                   xs, act, wgu_b, wd_b, stage_gu, stage_d, wsem, *, layer, n_chunks):
    i = pl.program_id(0)
    n_used = nu_ref[0]
    blk, d = xs.shape
    d_ff = act.shape[1]
    b = i
    live = i < n_used
    expert = be_ref[b]

    def weight_copies(e):
        return (pltpu.make_async_copy(wgu_hbm.at[layer, e], stage_gu, wsem.at[0]),
                pltpu.make_async_copy(wd_hbm.at[layer, e], stage_d, wsem.at[1]))

    @pl.when(i == 0)
    def _prologue():
        for c in weight_copies(expert):
            c.start()

    @pl.when(live)
    def _live():
        @pl.when(jnp.logical_or(b == 0, expert != be_ref[jnp.maximum(b - 1, 0)]))
        def _new_expert():
            for c in weight_copies(expert):
                c.wait()
            rows_gu = d // 8
            for c in range(8):
                rs = slice(c * rows_gu, (c + 1) * rows_gu)
                wgu_b[rs, :] = stage_gu[rs, :].astype(BF16)
            rows_d = d_ff // 4
            for c in range(4):
                rs = slice(c * rows_d, (c + 1) * rows_d)
                wd_b[rs, :] = stage_d[rs, :].astype(BF16)

            @pl.when(ne_ref[b] != expert)
            def _next_weights():
                for c in weight_copies(ne_ref[b]):
                    c.start()

        for s in range(n_chunks):
            xs[:, s * LANES:(s + 1) * LANES] = _rows_chunk(x_ref, s, blk, n_chunks).astype(xs.dtype)
        wide = 2 * LANES
        for c in range(d_ff // wide):
            a = jnp.dot(xs[...], wgu_b[:, c * wide:(c + 1) * wide], preferred_element_type=F32)
            gate = jnp.dot(xs[...], wgu_b[:, d_ff + c * wide:d_ff + (c + 1) * wide], preferred_element_type=F32)
            act[:, c * wide:(c + 1) * wide] = (_silu(a) * gate).astype(act.dtype)
        for c in range(d // (2 * wide)):
            y = jnp.dot(act[...], wd_b[:, c * 2 * wide:(c + 1) * 2 * wide], preferred_element_type=F32)
            for s in range(4):
                o_ref[pl.ds(4 * c + s, blk, stride=n_chunks), :] = y[:, s * LANES:(s + 1) * LANES]

    @pl.when(jnp.logical_not(live))
    def _idle():
        _zero_rows(o_ref)


def _experts(x_sorted, w_gu, w_down, layer, block_e, n_used, next_e):
    _, n_exp, d, two_ff = w_gu.shape
    d_ff = two_ff // 2
    blk = ROW_TILE
    n_chunks = d // LANES
    n_blocks = block_e.shape[0]
    kern = functools.partial(_expert_kernel, layer=layer, n_chunks=n_chunks)
    return pl.pallas_call(
        kern,
        out_shape=jax.ShapeDtypeStruct((n_blocks * blk * n_chunks, LANES), F32),
        grid_spec=pltpu.PrefetchScalarGridSpec(
            num_scalar_prefetch=3,
            grid=(n_blocks,),
            in_specs=[
                pl.BlockSpec((blk * n_chunks, LANES), lambda i, *_: (i, 0)),
                pl.BlockSpec(memory_space=pl.ANY),
                pl.BlockSpec(memory_space=pl.ANY),
            ],
            out_specs=pl.BlockSpec((blk * n_chunks, LANES), lambda i, *_: (i, 0)),
            scratch_shapes=[
                pltpu.VMEM((blk, d), BF16),
                pltpu.VMEM((blk, d_ff), BF16),
                pltpu.VMEM((d, two_ff), BF16),
                pltpu.VMEM((d_ff, d), BF16),
                pltpu.VMEM((d, two_ff), F32),
                pltpu.VMEM((d_ff, d), F32),
                pltpu.SemaphoreType.DMA((2,)),
            ],
        ),
        compiler_params=_params("arbitrary"),
        name="moe_experts",
    )(block_e, n_used, next_e, x_sorted, w_gu, w_down)


def _combine_kernel(dst0_ref, dstn_ref, ys_hbm, x_ref, gate_ref, g2_ref, ng_ref, nsh_ref, nsc_ref, *refs,
                    n_chunks, final_norm):
    if final_norm:
        o_ref, ybuf, sem = refs
        h_ref = o_ref
    else:
        o_ref, h_ref, ybuf, sem = refs
    i = pl.program_id(0)
    n = pl.num_programs(0)
    blk, d = x_ref.shape
    pitch = VMEM_ROW_PITCH
    slot = lax.rem(i, 2)
    tile_rows = blk * n_chunks

    def gather_start(idx_ref, buf):
        for r in range(blk):
            for k in range(TOP_K):
                src = pl.multiple_of(idx_ref[0, k * blk + r], n_chunks)
                pltpu.make_async_copy(ys_hbm.at[pl.ds(src, n_chunks)],
                                      ybuf.at[buf, k, pl.ds(r * pitch, n_chunks)],
                                      sem.at[buf]).start(priority=k % DMA_PRIORITIES)

    @pl.when(i == 0)
    def _prologue():
        gather_start(dst0_ref, 0)

    @pl.when(i + 1 < n)
    def _prefetch():
        gather_start(dstn_ref, 1 - slot)

    for k in range(TOP_K):
        pltpu.make_async_copy(ys_hbm.at[pl.ds(0, tile_rows)],
                              ybuf.at[slot, k, pl.ds(0, tile_rows)], sem.at[slot]).wait()
    gates = [jnp.broadcast_to(gate_ref[:, k:k + 1], (blk, LANES)) for k in range(TOP_K)]
    sq = jnp.zeros((blk, 1), F32)
    for s in range(n_chunks):
        cs = slice(s * LANES, (s + 1) * LANES)
        f = (_rows_chunk(ybuf.at[slot, 0], s, blk, pitch) * gates[0]
             + _rows_chunk(ybuf.at[slot, 1], s, blk, pitch) * gates[1])
        xo = x_ref[:, cs] + g2_ref[0, :, cs] * f
        o_ref[:, cs] = xo
        sq = sq + jnp.sum(xo * xo, axis=-1, keepdims=True)
    r = lax.rsqrt(sq * (1.0 / d) + NORM_EPS)
    for s in range(n_chunks):
        cs = slice(s * LANES, (s + 1) * LANES)
        hn = (o_ref[:, cs] * r) * ng_ref[:, cs] * (1.0 + nsc_ref[0, :, cs]) + nsh_ref[0, :, cs]
        h_ref[:, cs] = hn.astype(h_ref.dtype)


def _combine(ys, dest_tiles, x_tok, gates, g2, norm_g, norm_shift, norm_scale, n_rows, seq, final_norm):
    assert TOP_K == 2
    d = x_tok.shape[1]
    n_batch = g2.shape[0] - 1
    blk = ROW_TILE
    n_chunks = d // LANES
    tiles = n_rows // blk
    mod = _mod_index(seq // blk, n_batch)
    kern = functools.partial(_combine_kernel, n_chunks=n_chunks, final_norm=final_norm)
    smem_row = lambda index: pl.BlockSpec((None, 1, TOP_K * blk), index, memory_space=pltpu.SMEM)
    row_spec = pl.BlockSpec((blk, d), lambda i: (i, 0))
    if final_norm:
        out_shape, out_specs = jax.ShapeDtypeStruct((n_rows, d), F32), row_spec
    else:
        out_shape = (jax.ShapeDtypeStruct((n_rows, d), F32), jax.ShapeDtypeStruct((n_rows, d), BF16))
        out_specs = (row_spec, row_spec)
    return pl.pallas_call(
        kern,
        out_shape=out_shape,
        grid=(tiles,),
        in_specs=[
            smem_row(lambda i: (0, 0, 0)),
            smem_row(lambda i: (jnp.minimum(i + 1, tiles - 1), 0, 0)),
            pl.BlockSpec(memory_space=pl.ANY),
            pl.BlockSpec((blk, d), lambda i: (i, 0)),
            pl.BlockSpec((blk, TOP_K), lambda i: (i, 0)),
            pl.BlockSpec((1, 1, d), mod),
            pl.BlockSpec((1, d), lambda i: (0, 0)),
            pl.BlockSpec((1, 1, d), mod),
            pl.BlockSpec((1, 1, d), mod),
        ],
        out_specs=out_specs,
        scratch_shapes=[
            pltpu.VMEM((2, TOP_K, blk * VMEM_ROW_PITCH, LANES), F32),
            pltpu.SemaphoreType.DMA((2,)),
        ],
        compiler_params=_params("arbitrary"),
        name="moe_combine",
    )(dest_tiles, dest_tiles, ys, x_tok, gates.T, g2, norm_g.reshape(1, d), norm_shift, norm_scale)


def kernel(x, c, ctx, c_ctx, w_ada, b_ada, norm1_g, norm2_g, w_in, ret_decay, pool_w, pool_scale, sgu_w, sgu_b,
           w_branch, w_out, w_router, router_bias, w_gu, w_down, final_g):
    n_batch, seq, d = x.shape
    ctx_len = ctx.shape[1]
    depth = w_ada.shape[0]
    bw = d // 2
    t_x = n_batch * seq
    t_all = t_x + n_batch * ctx_len
    n_exp = w_router.shape[1]
    assert seq % ROW_TILE == 0 and ctx_len % ROW_TILE == 0 and ctx_len % RET_CHUNK == 0
    assert GRID_W & (GRID_W - 1) == 0 and seq % GRID_W == 0

    pad = (-(n_batch + 1)) % 8
    c_all = jnp.concatenate([c, c_ctx[None, :], jnp.zeros((pad, d), F32)], axis=0)
    ada = _ada(c_all, w_ada, b_ada)
    mods = [[m[:n_batch + 1, None, :] for m in jnp.split(ada[l], 6, axis=-1)] for l in range(depth)]
    no_mod = jnp.zeros((n_batch + 1, 1, d), F32)
    cos_t, sin_t = _rope_tables(n_batch, seq, ctx_len)
    w_branch_b, w_out_b = w_branch.astype(BF16), w_out.astype(BF16)

    x_main, x_ctx = x.reshape(t_x, d), ctx.reshape(n_batch * ctx_len, d)
    h = _norm(x_main, x_ctx, norm1_g[0], mods[0][0], mods[0][1], seq)
    out = None
    for l in range(depth):
        last = l == depth - 1
        sh1, sc1, g1, sh2, sc2, g2 = mods[l]
        rows_mix = t_x if last else t_all
        p = _in_proj(h, w_in, l, cos_t, sin_t, t_all, tm=_lane_tile(2304, t_all), tn=bw)
        lg = jax.nn.log_sigmoid(ret_decay[l].astype(F32)).T
        lg = jnp.broadcast_to(lg[:, :, None], lg.shape + (RET_CHUNK,))
        ret = _retention(p, lg, n_batch, seq, ctx_len)
        pooled = _pool(p, pool_w[l].astype(BF16), pool_scale[l], 4 * bw, n_batch, seq, ctx_len, not last)
        merged = _merge((ret, pooled), sgu_w[l], sgu_b[l], w_branch_b, l, p, 5 * bw, 7 * bw, rows_mix)
        x_tok, eid, gates = _matmul_residual(merged, w_out_b, l, x_main, x_ctx if rows_mix > t_x else None, g1,
                                             norm2_g[l], sh2, sc2, w_router, router_bias, rows_mix, seq)
        dest_tiles, block_e, n_used, next_e, pad_start, pad_cnt = _route_plan(eid, n_exp, d // LANES)
        x_sorted = _dispatch(x_tok, norm2_g[l], sh2, sc2, dest_tiles, pad_start, pad_cnt, n_used,
                             rows_mix, block_e.shape[0] * ROW_TILE, seq)
        ys = _experts(x_sorted, w_gu, w_down, l, block_e, n_used, next_e)
        if last:
            out = _combine(ys, dest_tiles, x_tok, gates, g2, final_g, no_mod, no_mod, rows_mix, seq, True)
        else:
            x_main, h = _combine(ys, dest_tiles, x_tok, gates, g2, norm1_g[l + 1], mods[l + 1][0], mods[l + 1][1],
                                 rows_mix, seq, False)
            x_ctx = x_main[t_x:]
    return out.reshape(n_batch, seq, d)
```

```python
import functools

import numpy as np
import jax
import jax.numpy as jnp
from jax import lax
from jax.experimental import pallas as pl
from jax.experimental.pallas import tpu as pltpu

F32 = jnp.float32
BF16 = jnp.bfloat16

GRID_W = 64
RET_DK = 128
POOL_WINDOWS = (2, 4, 8, 16)
SGU_CHUNK = 128
N_EXPERT_GROUPS = 4
TOP_K = 2
ROPE_BASE = 10000.0
NORM_EPS = 1e-6

V7X_VMEM_LIMIT_BYTES = 60 * 1024 * 1024
LANES = 128

ROW_TILE = 256
RET_CHUNK = 256
POOL_HALO = 512


def _tile(pref, *sizes):
    t = pref
    while any(s % t for s in sizes):
        t //= 2
    return t


def _lane_tile(cap, size):
    return max(t for t in range(LANES, cap + 1, LANES) if size % t == 0)


def _params(*sem):
    return pltpu.CompilerParams(dimension_semantics=sem, vmem_limit_bytes=V7X_VMEM_LIMIT_BYTES)


def _silu(x):
    return x * jax.nn.sigmoid(x)


def _ada_kernel(c_ref, w_ref, b_ref, o_ref):
    a = _silu(c_ref[...]).astype(BF16)
    o_ref[...] = jnp.dot(a, w_ref[...].astype(BF16), preferred_element_type=F32) + b_ref[...]


def _ada(c_all, w_ada, b_ada, tn=1024):
    depth, d, n = w_ada.shape
    rows = c_all.shape[0]
    return pl.pallas_call(
        _ada_kernel,
        out_shape=jax.ShapeDtypeStruct((depth, rows, n), F32),
        grid=(depth, n // tn),
        in_specs=[
            pl.BlockSpec((rows, d), lambda l, j: (0, 0)),
            pl.BlockSpec((None, d, tn), lambda l, j: (l, 0, j)),
            pl.BlockSpec((None, 1, tn), lambda l, j: (l, 0, j)),
        ],
        out_specs=pl.BlockSpec((None, rows, tn), lambda l, j: (l, 0, j)),
        compiler_params=_params("arbitrary", "arbitrary"),
        name="ada_proj",
    )(c_all, w_ada, b_ada.reshape(depth, 1, n))


def _modulate(x, g, shift, scale):
    r = lax.rsqrt(jnp.mean(x * x, axis=-1, keepdims=True) + NORM_EPS)
    return (x * r) * g * (1.0 + scale) + shift


def _two_source_specs(block, main_tiles, grid_rank):
    main = lambda i: jnp.minimum(i, main_tiles - 1)
    ctx = lambda i: jnp.maximum(i - main_tiles, 0)
    if grid_rank == 1:
        return pl.BlockSpec(block, lambda i: (main(i), 0)), pl.BlockSpec(block, lambda i: (ctx(i), 0))
    return pl.BlockSpec(block, lambda i, j: (main(i), j)), pl.BlockSpec(block, lambda i, j: (ctx(i), j))


def _two_source_pick(i, main_tiles, main_ref, ctx_ref):
    return jnp.where(i < main_tiles, main_ref[...], ctx_ref[...])


def _norm_kernel(xm_ref, xc_ref, g_ref, sh_ref, sc_ref, o_ref, *, main_tiles):
    x = _two_source_pick(pl.program_id(0), main_tiles, xm_ref, xc_ref)
    o_ref[...] = _modulate(x, g_ref[...], sh_ref[0], sc_ref[0]).astype(o_ref.dtype)


def _mod_index(tiles_per_batch, n_batch):
    return lambda i: (jnp.minimum(i // tiles_per_batch, n_batch), 0, 0)


def _norm(x_main, x_ctx, g, shift, scale, seq):
    d = x_main.shape[1]
    n_rows = x_main.shape[0] + x_ctx.shape[0]
    n_batch = shift.shape[0] - 1
    main_tiles = x_main.shape[0] // ROW_TILE
    mod = _mod_index(seq // ROW_TILE, n_batch)
    xm_spec, xc_spec = _two_source_specs((ROW_TILE, d), main_tiles, 1)
    return pl.pallas_call(
        functools.partial(_norm_kernel, main_tiles=main_tiles),
        out_shape=jax.ShapeDtypeStruct((n_rows, d), BF16),
        grid=(n_rows // ROW_TILE,),
        in_specs=[
            xm_spec,
            xc_spec,
            pl.BlockSpec((1, d), lambda i: (0, 0)),
            pl.BlockSpec((1, 1, d), mod),
            pl.BlockSpec((1, 1, d), mod),
        ],
        out_specs=pl.BlockSpec((ROW_TILE, d), lambda i: (i, 0)),
        compiler_params=_params("arbitrary"),
        name="mod_norm",
    )(x_main, x_ctx, g.reshape(1, d), shift, scale)


def _in_proj_kernel(a_ref, b_ref, cos_ref, sin_ref, o_ref, wb, *, rope_tiles, qscale):
    j = pl.program_id(0)

    @pl.when(pl.program_id(1) == 0)
    def _cast():
        rows = wb.shape[0] // 8
        for c in range(8):
            wb[c * rows:(c + 1) * rows, :] = b_ref[c * rows:(c + 1) * rows, :].astype(wb.dtype)

    @pl.when(j >= rope_tiles)
    def _plain():
        wide = o_ref.shape[1] // 2
        for c in range(2):
            cs = slice(c * wide, (c + 1) * wide)
            o_ref[:, cs] = jnp.dot(a_ref[...], wb[:, cs], preferred_element_type=F32).astype(o_ref.dtype)

    @pl.when(j < rope_tiles)
    def _rotated():
        scale = jnp.where(j == 0, qscale, 1.0)
        pair = 2 * RET_DK
        half_rows = o_ref.shape[0] // 2
        for r in range(2):
            rs = slice(r * half_rows, (r + 1) * half_rows)
            cosv = cos_ref[rs, :]
            sinv = sin_ref[rs, :]
            for c in range(o_ref.shape[1] // pair):
                t2 = jnp.dot(a_ref[rs, :], wb[:, c * pair:(c + 1) * pair], preferred_element_type=F32)
                for hh in range(2):
                    t = t2[:, hh * RET_DK:(hh + 1) * RET_DK]
                    hs = slice(c * pair + hh * RET_DK, c * pair + (hh + 1) * RET_DK)
                    rotated = t * cosv + pltpu.roll(t, RET_DK // 2, 1) * sinv
                    o_ref[rs, hs] = (rotated * scale).astype(o_ref.dtype)


def _in_proj(a, b, layer, cos_t, sin_t, n_rows, tm, tn):
    k = a.shape[1]
    n = b.shape[2]
    rope_tiles = 2
    table = lambda j, i: (jnp.where(j < rope_tiles, i, 0), 0)
    kern = functools.partial(_in_proj_kernel, rope_tiles=rope_tiles, qscale=float(RET_DK) ** -0.5)
    return pl.pallas_call(
        kern,
        out_shape=jax.ShapeDtypeStruct((n_rows, n), BF16),
        grid=(n // tn, n_rows // tm),
        in_specs=[
            pl.BlockSpec((tm, k), lambda j, i: (i, 0)),
            pl.BlockSpec((None, k, tn), lambda j, i: (layer, 0, j)),
            pl.BlockSpec((tm, RET_DK), table),
            pl.BlockSpec((tm, RET_DK), table),
        ],
        out_specs=pl.BlockSpec((tm, tn), lambda j, i: (i, j)),
        scratch_shapes=[pltpu.VMEM((k, tn), a.dtype)],
        compiler_params=_params("arbitrary", "arbitrary"),
        name="in_proj",
    )(a, b, cos_t, sin_t)


def _mm_residual_kernel(a_ref, b_ref, g_ref, ng_ref, nsh_ref, nsc_ref, wr_ref, br_ref, *refs, main_tiles):
    *x_refs, o_ref, eid_ref, gate_ref = refs
    y = g_ref[0] * jnp.dot(a_ref[...], b_ref[...], preferred_element_type=F32)
    if len(x_refs) == 1:
        o_ref[...] = x_refs[0][...] + y
    else:
        is_main = pl.program_id(0) < main_tiles

        @pl.when(is_main)
        def _main():
            o_ref[...] = x_refs[0][...] + y

        @pl.when(jnp.logical_not(is_main))
        def _ctx():
            o_ref[...] = x_refs[1][...] + y

    h = _modulate(o_ref[...], ng_ref[...], nsh_ref[0], nsc_ref[0]).astype(BF16)
    _route_tokens(h, wr_ref, br_ref, eid_ref, gate_ref)


def _matmul_residual(a, b, layer, x_main, x_ctx, gate, norm_g, norm_shift, norm_scale, w_router, router_bias,
                     n_rows, seq):
    k = a.shape[1]
    n = b.shape[2]
    n_exp = w_router.shape[1]
    n_batch = gate.shape[0] - 1
    tm = _tile(512, n_rows, seq) if x_ctx is None else _tile(512, n_rows, seq, x_ctx.shape[0])
    tiles_per_batch = seq // tm
    main_tiles = n_batch * tiles_per_batch
    mod = _mod_index(tiles_per_batch, n_batch)
    if x_ctx is None:
        x_specs, x_args = [pl.BlockSpec((tm, n), lambda i: (i, 0))], [x_main]
    else:
        x_specs, x_args = list(_two_source_specs((tm, n), main_tiles, 1)), [x_main, x_ctx]
    return pl.pallas_call(
        functools.partial(_mm_residual_kernel, main_tiles=main_tiles),
        out_shape=(
            jax.ShapeDtypeStruct((n_rows, n), F32),
            jax.ShapeDtypeStruct((TOP_K, n_rows), jnp.int32),
            jax.ShapeDtypeStruct((TOP_K, n_rows), F32),
        ),
        grid=(n_rows // tm,),
        in_specs=[
            pl.BlockSpec((tm, k), lambda i: (i, 0)),
            pl.BlockSpec((None, k, n), lambda i: (layer, 0, 0)),
            pl.BlockSpec((1, 1, n), mod),
            pl.BlockSpec((1, n), lambda i: (0, 0)),
            pl.BlockSpec((1, 1, n), mod),
            pl.BlockSpec((1, 1, n), mod),
            pl.BlockSpec((n_exp, n), lambda i: (0, 0)),
            pl.BlockSpec((n_exp, 1), lambda i: (0, 0)),
        ] + x_specs,
        out_specs=(
            pl.BlockSpec((tm, n), lambda i: (i, 0)),
            pl.BlockSpec((TOP_K, tm), lambda i: (0, i)),
            pl.BlockSpec((TOP_K, tm), lambda i: (0, i)),
        ),
        compiler_params=_params("arbitrary"),
        name="out_proj_residual",
    )(a, b, gate, norm_g.reshape(1, n), norm_shift, norm_scale, w_router.T.astype(BF16),
      router_bias.reshape(n_exp, 1), *x_args)


def _spatial_gate(u_ref, v_ref, w_ref, b_ref, o_ref):
    n_groups = w_ref.shape[0]
    gw = o_ref.shape[1] // n_groups
    v = v_ref[...].astype(F32)
    mu = jnp.mean(v, axis=-1, keepdims=True)
    vc = v - mu
    var = jnp.mean(vc * vc, axis=-1, keepdims=True)
    vn = (vc * lax.rsqrt(var + NORM_EPS)).astype(BF16)
    for kc in range(o_ref.shape[0] // SGU_CHUNK):
        rs = slice(kc * SGU_CHUNK, (kc + 1) * SGU_CHUNK)
        for g in range(n_groups):
            cs = slice(g * gw, (g + 1) * gw)
            mixed = jnp.dot(w_ref[g], vn[rs, cs], preferred_element_type=F32) + b_ref[:, g:g + 1]
            o_ref[rs, cs] = (u_ref[rs, cs].astype(F32) * mixed).astype(o_ref.dtype)


def _merge_kernel(ret_ref, pool_main_ref, pool_ctx_ref, u_ref, v_ref, ws_ref, bs_ref, *refs,
                  n_branch, gate_parts, main_tiles):
    b_refs = refs[:n_branch]
    g_refs = refs[n_branch:-3]
    o_ref, sgu_out, pool_tile = refs[-3:]
    is_main = pl.program_id(0) < main_tiles

    @pl.when(is_main)
    def _pool_main():
        pool_tile[...] = pool_main_ref[...]

    @pl.when(jnp.logical_not(is_main))
    def _pool_ctx():
        pool_tile[...] = pool_ctx_ref[...]

    _spatial_gate(u_ref, v_ref, ws_ref, bs_ref, sgu_out)
    a_refs = (ret_ref, pool_tile, sgu_out)
    wide = 4 * LANES
    part = o_ref.shape[1] // gate_parts
    for c in range(o_ref.shape[1] // wide):
        cs = slice(c * wide, (c + 1) * wide)
        gs = slice(c * wide % part, c * wide % part + wide)
        acc = None
        for s in range(n_branch):
            g_ref = g_refs[s * gate_parts + c * wide // part]
            y = (jax.nn.sigmoid(g_ref[:, gs].astype(F32))
                 * jnp.dot(a_refs[s][...], b_refs[s][:, cs], preferred_element_type=F32))
            acc = y if acc is None else acc + y
        o_ref[:, cs] = acc.astype(o_ref.dtype)


def _merge(ret, pooled, pooled_ctx, sgu_w, sgu_b, w_branch, layer, p, sgu_col0, gate_col0, n_rows):
    n_branch = 3
    kb = ret.shape[1]
    tm = _tile(512, n_rows, pooled.shape[0], pooled_ctx.shape[0])
    main_tiles = pooled.shape[0] // tm
    n = w_branch.shape[3]
    gate_parts = 2
    gw = n // gate_parts
    assert gate_col0 % gw == 0 and sgu_col0 % kb == 0
    n_groups = sgu_w.shape[0]
    a_specs = [pl.BlockSpec((tm, kb), lambda i: (i, 0))] + list(_two_source_specs((tm, kb), main_tiles, 1))
    a_specs += [
        pl.BlockSpec((tm, kb), lambda i: (i, sgu_col0 // kb)),
        pl.BlockSpec((tm, kb), lambda i: (i, sgu_col0 // kb + 1)),
        pl.BlockSpec(sgu_w.shape, lambda i: (0, 0, 0)),
        pl.BlockSpec((SGU_CHUNK, n_groups), lambda i: (0, 0)),
    ]
    b_specs = [pl.BlockSpec((None, None, kb, n), functools.partial(lambda i, s: (layer, s, 0, 0), s=s),
                            pipeline_mode=pl.Buffered(1))
               for s in range(n_branch)]
    g_specs = [
        pl.BlockSpec((tm, gw), functools.partial(lambda i, c: (i, c), c=(gate_col0 + s * n) // gw + part))
        for s in range(n_branch) for part in range(gate_parts)
    ]
    return pl.pallas_call(
        functools.partial(_merge_kernel, n_branch=n_branch, gate_parts=gate_parts, main_tiles=main_tiles),
        out_shape=jax.ShapeDtypeStruct((n_rows, n), BF16),
        grid=(n_rows // tm,),
        in_specs=a_specs + b_specs + g_specs,
        out_specs=pl.BlockSpec((tm, n), lambda i: (i, 0)),
        scratch_shapes=[pltpu.VMEM((tm, kb), BF16), pltpu.VMEM((tm, kb), BF16)],
        compiler_params=_params("arbitrary"),
        name="branch_merge",
    )(ret, pooled, pooled_ctx, p, p, sgu_w.astype(BF16), sgu_b.T, *([w_branch] * n_branch),
      *([p] * (n_branch * gate_parts)))


def _retention_kernel(q_ref, k_ref, v_ref, g_ref, lg_ref, o_ref,
                      sf_all, s_run, dmat, rw, *, n_heads, nc_ctx, nc_x):
    pss = pl.program_id(1)
    j = pl.program_id(2)
    c = RET_CHUNK
    dk = RET_DK

    @pl.when(jnp.logical_and(pss == 0, j == 0))
    def _tables():
        row = lax.broadcasted_iota(jnp.int32, (c, c), 0)
        col = lax.broadcasted_iota(jnp.int32, (c, c), 1)
        diff = (row - col).astype(F32)
        pos = lax.broadcasted_iota(jnp.int32, (c, dk), 0).astype(F32)
        for h in range(n_heads):
            lgf = lg_ref[h, 0:1, :]
            lgb = lg_ref[h, 1:2, :]
            fwd = jnp.where(diff >= 0, jnp.exp(jnp.maximum(diff, 0.0) * lgf), 0.0)
            bwd = jnp.where(diff <= 0, jnp.exp(jnp.maximum(-diff, 0.0) * lgb), 0.0)
            dmat[h] = fwd + bwd
            lgf1 = lgf[:, :dk]
            lgb1 = lgb[:, :dk]
            rw[h, 0] = jnp.exp((pos + 1.0) * lgf1)
            rw[h, 1] = jnp.exp((c - pos) * lgb1)
            rw[h, 2] = jnp.exp((c - 1.0 - pos) * lgf1)
            rw[h, 3] = jnp.exp(pos * lgb1)

    @pl.when(j == 0)
    def _reset():
        s_run[...] = jnp.zeros_like(s_run)

    def state_update(h, kk, vv, direction):
        wv = (vv.astype(F32) * rw[h, 2 + direction]).astype(BF16)
        kv = lax.dot_general(kk, wv, (((0,), (0,)), ((), ())), preferred_element_type=F32)
        decay = jnp.exp(float(c) * lg_ref[h, direction:direction + 1, 0:dk])
        s_run[h] = decay * s_run[h] + kv

    @pl.when(pss == 0)
    def _forward():
        for h in range(n_heads):
            cs = slice(h * dk, (h + 1) * dk)
            sf_all[j, h] = s_run[h]
            state_update(h, k_ref[:, cs], v_ref[:, cs], 0)

    @pl.when(pss == 1)
    def _backward():
        idx = jnp.where(j < nc_ctx, nc_ctx - 1 - j, nc_ctx + nc_x - 1 - (j - nc_ctx))
        for h in range(n_heads):
            cs = slice(h * dk, (h + 1) * dk)
            kk = k_ref[:, cs]
            qq = q_ref[:, cs]
            vv = v_ref[:, cs]
            scores = lax.dot_general(qq, kk, (((1,), (1,)), ((), ())), preferred_element_type=F32)
            inner = jnp.dot((scores * dmat[h]).astype(BF16), vv, preferred_element_type=F32)
            states = jnp.concatenate([sf_all[idx, h], s_run[h]], axis=1).astype(BF16)
            cross = jnp.dot(qq, states, preferred_element_type=F32)
            y = inner + rw[h, 0] * cross[:, :dk] + rw[h, 1] * cross[:, dk:]
            mu = jnp.mean(y, axis=-1, keepdims=True)
            yc = y - mu
            var = jnp.mean(yc * yc, axis=-1, keepdims=True)
            yn = yc * lax.rsqrt(var + NORM_EPS)
            o_ref[:, cs] = (yn * _silu(g_ref[:, cs].astype(F32))).astype(o_ref.dtype)
            state_update(h, kk, vv, 1)


def _retention(p, lg, n_batch, seq, ctx_len):
    bw = lg.shape[0] * RET_DK
    n_heads = lg.shape[0]
    c = RET_CHUNK
    nc_x = seq // c
    nc_ctx = ctx_len // c
    nc = nc_x + nc_ctx
    t = n_batch * (seq + ctx_len)

    def fwd_idx(pss, j):
        bwd = jnp.where(j < nc_ctx, nc_ctx - 1 - j, nc_ctx + nc_x - 1 - (j - nc_ctx))
        return jnp.where(pss == 0, j, bwd)

    def row_block(b, idx):
        return jnp.where(idx < nc_ctx, n_batch * nc_x + b * nc_ctx + idx, b * nc_x + (idx - nc_ctx))

    def kv_map(col):
        return lambda b, pss, j: (row_block(b, fwd_idx(pss, j)), col)

    def qg_map(col):
        return lambda b, pss, j: (row_block(b, fwd_idx(1, jnp.where(pss == 0, 0, j))), col)

    kern = functools.partial(_retention_kernel, n_heads=n_heads, nc_ctx=nc_ctx, nc_x=nc_x)
    return pl.pallas_call(
        kern,
        out_shape=jax.ShapeDtypeStruct((t, bw), BF16),
        grid=(n_batch, 2, nc),
        in_specs=[
            pl.BlockSpec((c, bw), qg_map(0)),
            pl.BlockSpec((c, bw), kv_map(1)),
            pl.BlockSpec((c, bw), kv_map(2)),
            pl.BlockSpec((c, bw), qg_map(3)),
            pl.BlockSpec((n_heads, 2, c), lambda b, pss, j: (0, 0, 0)),
        ],
        out_specs=pl.BlockSpec((c, bw), qg_map(0)),
        scratch_shapes=[
            pltpu.VMEM((nc, n_heads, RET_DK, RET_DK), F32),
            pltpu.VMEM((n_heads, RET_DK, RET_DK), F32),
            pltpu.VMEM((n_heads, c, c), F32),
            pltpu.VMEM((n_heads, 4, c, RET_DK), F32),
        ],
        compiler_params=_params("arbitrary", "arbitrary", "arbitrary"),
        name="retention",
    )(p, p, p, p, lg)


def _rope_tables(n_batch, seq, ctx_len):
    t = np.arange(seq)
    rows = (t // GRID_W).astype(np.float32)
    cols = (t % GRID_W).astype(np.float32)
    quarter = RET_DK // 4
    inv = jnp.asarray(ROPE_BASE, F32) ** (-jnp.arange(quarter, dtype=F32) / quarter)
    ang = jnp.concatenate([jnp.asarray(rows)[:, None] * inv, jnp.asarray(cols)[:, None] * inv], axis=-1)
    cos, sin = jnp.cos(ang), jnp.sin(ang)
    cos_x = jnp.tile(jnp.concatenate([cos, cos], axis=-1), (n_batch, 1))
    sin_x = jnp.tile(jnp.concatenate([-sin, sin], axis=-1), (n_batch, 1))
    cos_t = jnp.concatenate([cos_x, jnp.ones((n_batch * ctx_len, RET_DK), F32)], axis=0)
    sin_t = jnp.concatenate([sin_x, jnp.zeros((n_batch * ctx_len, RET_DK), F32)], axis=0)
    return cos_t, sin_t


def _pool_kernel(x_ref, win_ref, w_ref, sc_ref, o_ref, xpad, *, n_tok, halo, grid_rows):
    group = pl.program_id(1)
    slab = ROW_TILE
    if halo:
        xpad[0:halo, :] = jnp.zeros((halo, xpad.shape[1]), xpad.dtype)
        xpad[halo + n_tok:, :] = jnp.zeros((halo, xpad.shape[1]), xpad.dtype)
    xpad[halo:halo + n_tok, :] = x_ref[...]

    def mix(half, lo, width):
        for s in range(n_tok // slab):
            total = jnp.dot(win_ref[:, lo:lo + width], xpad[s * slab + lo:s * slab + lo + width, :],
                            preferred_element_type=F32)
            tok = s * slab + lax.broadcasted_iota(jnp.int32, total.shape, 0)
            if grid_rows is None:
                count = jnp.minimum(tok + half, n_tok) - jnp.maximum(tok - half, 0)
            else:
                col = jnp.bitwise_and(tok, GRID_W - 1)
                row = jnp.right_shift(tok, GRID_W.bit_length() - 1)
                count = ((jnp.minimum(col + half, GRID_W) - jnp.maximum(col - half, 0))
                         * (jnp.minimum(row + half, grid_rows) - jnp.maximum(row - half, 0)))
            mean = total * (1.0 / count.astype(F32))
            d = (mean - x_ref[s * slab:(s + 1) * slab, :].astype(F32)).astype(BF16)
            y = jnp.dot(d, w_ref[...], preferred_element_type=F32) * sc_ref[...]
            o_ref[s * slab:(s + 1) * slab, :] = y.astype(o_ref.dtype)

    if grid_rows is None:
        mix(jnp.left_shift(1, group), 0, win_ref.shape[1])
    else:
        for k, window in enumerate(POOL_WINDOWS):
            reach = window // 2 * GRID_W
            lo = (halo - reach) // LANES * LANES
            hi = -(-(halo + slab + reach) // LANES) * LANES

            @pl.when(group == k)
            def _group(window=window, lo=lo, hi=hi):
                mix(window // 2, lo, hi - lo)


def _pool_windows_2d():
    a = np.arange(ROW_TILE)
    s = np.arange(ROW_TILE + 2 * POOL_HALO)
    ra, ca = a // GRID_W, a % GRID_W
    rs, cs = s // GRID_W - POOL_HALO // GRID_W, s % GRID_W
    out = []
    for w in POOL_WINDOWS:
        h = w // 2
        dr = rs[None, :] - ra[:, None]
        dc = cs[None, :] - ca[:, None]
        out.append(((dr >= -h) & (dr <= h - 1) & (dc >= -h) & (dc <= h - 1)).astype(np.float32))
    return jnp.asarray(np.stack(out), BF16)


def _pool_windows_1d(n):
    a = np.arange(n)
    out = []
    for w in POOL_WINDOWS:
        h = w // 2
        ds = a[None, :] - a[:, None]
        out.append(((ds >= -h) & (ds <= h - 1)).astype(np.float32))
    return jnp.asarray(np.stack(out), BF16)


def _pool(p, pool_w, pool_scale, col0, n_batch, seq, ctx_len, with_ctx):
    n_groups, gw, _ = pool_w.shape
    bw = n_groups * gw
    assert POOL_HALO >= max(POOL_WINDOWS) // 2 * GRID_W and gw == ROW_TILE
    scale = pool_scale.reshape(1, bw)
    cb = col0 // gw

    def call(n_tok, halo, grid_rows, row0_blocks, windows):
        kern = functools.partial(_pool_kernel, n_tok=n_tok, halo=halo, grid_rows=grid_rows)
        return pl.pallas_call(
            kern,
            out_shape=jax.ShapeDtypeStruct((n_batch * n_tok, bw), BF16),
            grid=(n_batch, n_groups),
            in_specs=[
                pl.BlockSpec((n_tok, gw), lambda b, g: (row0_blocks + b, cb + g)),
                pl.BlockSpec((None,) + windows.shape[1:], lambda b, g: (g, 0, 0)),
                pl.BlockSpec((None, gw, gw), lambda b, g: (g, 0, 0)),
                pl.BlockSpec((1, gw), lambda b, g: (0, g)),
            ],
            out_specs=pl.BlockSpec((n_tok, gw), lambda b, g: (b, g)),
            scratch_shapes=[pltpu.VMEM((n_tok + 2 * halo, gw), BF16)],
            compiler_params=_params("arbitrary", "arbitrary"),
            name="pool_mix_ctx" if grid_rows is None else "pool_mix",
        )(p, windows, pool_w, scale)

    out = call(seq, POOL_HALO, seq // GRID_W, 0, _pool_windows_2d())
    out_ctx = call(ctx_len, 0, None, n_batch * seq // ctx_len, _pool_windows_1d(ctx_len)) if with_ctx else out
    return out, out_ctx


VMEM_ROW_PITCH = 24
DMA_PRIORITIES = 2


def _rows_store(dst_ref, v, pitch):
    rows = v.shape[0]
    for s in range(v.shape[1] // LANES):
        dst_ref[pl.ds(s, rows, stride=pitch), :] = v[:, s * LANES:(s + 1) * LANES].astype(dst_ref.dtype)


def _zero_rows(ref, piece=512):
    for c in range(ref.shape[0] // piece):
        ref[c * piece:(c + 1) * piece, :] = jnp.zeros((piece, ref.shape[1]), ref.dtype)


def _rows_chunk(src_ref, s, rows, pitch):
    return src_ref[pl.ds(s, rows, stride=pitch), :]


def _route_tokens(h, wr_ref, br_ref, eid_ref, gate_ref):
    n_exp = wr_ref.shape[0]
    logits = lax.dot_general(wr_ref[...], h, (((1,), (1,)), ((), ())), preferred_element_type=F32) + br_ref[...]
    e = jnp.exp(logits - jnp.max(logits, axis=0, keepdims=True))
    probs = e / jnp.sum(e, axis=0, keepdims=True)
    per = n_exp // N_EXPERT_GROUPS
    n_tok = probs.shape[1]
    lane_iota = lax.broadcasted_iota(jnp.int32, (per, n_tok), 0)
    best = None
    for grp in range(N_EXPERT_GROUPS):
        pg = probs[grp * per:(grp + 1) * per, :]
        m1 = jnp.max(pg, axis=0, keepdims=True)
        i1 = jnp.min(jnp.where(pg == m1, lane_iota, per), axis=0, keepdims=True)
        rest = jnp.where(lane_iota == i1, -1.0, pg)
        m2 = jnp.max(rest, axis=0, keepdims=True)
        i2 = jnp.min(jnp.where(rest == m2, lane_iota, per), axis=0, keepdims=True)
        cand = (m1 + m2, m1, m2, i1 + grp * per, i2 + grp * per)
        if best is None:
            best = cand
        else:
            take = cand[0] > best[0]
            best = tuple(jnp.where(take, cn, bs) for cn, bs in zip(cand, best))
    _, m1, m2, e1, e2 = best
    denom = m1 + m2
    eid_ref[0:1, :] = e1
    eid_ref[1:2, :] = e2
    gate_ref[0:1, :] = m1 / denom
    gate_ref[1:2, :] = m2 / denom


def _route_plan(eid, n_exp, n_chunks):
    n_tok = eid.shape[1]
    blk = ROW_TILE
    n_assign = TOP_K * n_tok
    tiles = n_tok // blk
    flat_e = eid.reshape(-1)
    onehot = flat_e[:, None] == jnp.arange(n_exp, dtype=jnp.int32)[None, :]
    oh = onehot.astype(BF16).reshape(n_assign // blk, blk, n_exp)
    tril = jnp.tril(jnp.ones((blk, blk), BF16))
    within = jnp.einsum("ij,cjk->cik", tril, oh, preferred_element_type=F32)
    totals = within[:, -1, :]
    offsets = jnp.cumsum(totals, axis=0) - totals
    csum = (within + offsets[:, None, :]).reshape(n_assign, n_exp).astype(jnp.int32)
    counts = csum[-1]
    rank = jnp.sum(jnp.where(onehot, csum, 0), axis=1) - 1
    padded = ((counts + blk - 1) // blk) * blk
    ends = jnp.cumsum(padded)
    start = ends - padded
    dest = (jnp.sum(jnp.where(onehot, start[None, :], 0), axis=1) + rank).astype(jnp.int32)
    dest_tiles = (dest * n_chunks).reshape(TOP_K, tiles, blk).transpose(1, 0, 2).reshape(tiles, 1, TOP_K * blk)
    n_blocks = -(-(n_assign + n_exp * (blk - 1)) // blk)
    block_start = jnp.arange(n_blocks, dtype=jnp.int32) * blk
    block_e = jnp.minimum(jnp.sum((block_start[:, None] >= ends[None, :]).astype(jnp.int32), axis=1), n_exp - 1)
    n_used = ends[-1] // blk
    live = jnp.arange(n_blocks) < n_used
    later = jnp.logical_and(live[None, :], block_e[None, :] > block_e[:, None])
    next_e = jnp.min(jnp.where(later, block_e[None, :], n_exp), axis=1)
    next_e = jnp.where(next_e == n_exp, block_e, next_e)
    i32 = lambda v: v.astype(jnp.int32)
    pad_start = (start + counts) * n_chunks
    pad_cnt = padded - counts
    return dest_tiles, i32(block_e), i32(n_used).reshape(1), i32(next_e), i32(pad_start), i32(pad_cnt)


def _dispatch_kernel(ps_ref, pc_ref, nu_ref, dst_ref, x_ref, g_ref, sh_ref, sc_ref, out_hbm,
                     stage, zbuf, sem, zsem, *, n_chunks):
    i = pl.program_id(0)
    n = pl.num_programs(0)
    blk = x_ref.shape[0]
    pitch = VMEM_ROW_PITCH
    slot = lax.rem(i, 2)
    tile_rows = blk * n_chunks
    n_blocks = out_hbm.shape[0] // tile_rows

    def scatter_wait(buf):
        for _ in range(TOP_K):
            pltpu.make_async_copy(stage.at[buf, pl.ds(0, tile_rows)],
                                  out_hbm.at[pl.ds(0, tile_rows)], sem.at[buf]).wait()

    @pl.when(i == 0)
    def _fill_padding():
        _zero_rows(zbuf)

        def pad_row(e, r):
            dst = pl.multiple_of(ps_ref[e] + r * n_chunks, n_chunks)
            return pltpu.make_async_copy(zbuf.at[pl.ds(0, n_chunks)], out_hbm.at[pl.ds(dst, n_chunks)], zsem)

        def idle_block(b):
            dst = pl.multiple_of(b * tile_rows, tile_rows)
            return pltpu.make_async_copy(zbuf, out_hbm.at[pl.ds(dst, tile_rows)], zsem)

        def for_all(op):
            def per_expert(e, carry):
                def per_row(r, c):
                    op(pad_row(e, r))
                    return c
                return lax.fori_loop(0, pc_ref[e], per_row, carry)
            lax.fori_loop(0, ps_ref.shape[0], per_expert, 0)

            def per_block(b, c):
                op(idle_block(b))
                return c
            lax.fori_loop(nu_ref[0], n_blocks, per_block, 0)

        for_all(lambda c: c.start())
        for_all(lambda c: c.wait())

    @pl.when(i >= 2)
    def _reuse_stage():
        scatter_wait(slot)

    _rows_store(stage.at[slot], _modulate(x_ref[...], g_ref[...], sh_ref[0], sc_ref[0]), pitch)
    for r in range(blk):
        for k in range(TOP_K):
            dst = pl.multiple_of(dst_ref[0, k * blk + r], n_chunks)
            pltpu.make_async_copy(stage.at[slot, pl.ds(r * pitch, n_chunks)],
                                  out_hbm.at[pl.ds(dst, n_chunks)], sem.at[slot]).start(priority=k % DMA_PRIORITIES)

    @pl.when(i == n - 1)
    def _drain():
        scatter_wait(slot)

        @pl.when(i >= 1)
        def _previous():
            scatter_wait(1 - slot)


def _dispatch(x_tok, g, shift, scale, dest_tiles, pad_start, pad_cnt, n_used, n_rows, n_sorted, seq):
    d = x_tok.shape[1]
    n_chunks = d // LANES
    n_batch = shift.shape[0] - 1
    blk = ROW_TILE
    mod = _mod_index(seq // blk, n_batch)
    kern = functools.partial(_dispatch_kernel, n_chunks=n_chunks)
    return pl.pallas_call(
        kern,
        out_shape=jax.ShapeDtypeStruct((n_sorted * n_chunks, LANES), F32),
        grid_spec=pltpu.PrefetchScalarGridSpec(
            num_scalar_prefetch=3,
            grid=(n_rows // blk,),
            in_specs=[
                pl.BlockSpec((None, 1, TOP_K * blk), lambda i, *_: (i, 0, 0), memory_space=pltpu.SMEM),
                pl.BlockSpec((blk, d), lambda i, *_: (i, 0)),
                pl.BlockSpec((1, d), lambda i, *_: (0, 0)),
                pl.BlockSpec((1, 1, d), lambda i, *_: mod(i)),
                pl.BlockSpec((1, 1, d), lambda i, *_: mod(i)),
            ],
            out_specs=pl.BlockSpec(memory_space=pl.ANY),
            scratch_shapes=[
                pltpu.VMEM((2, blk * VMEM_ROW_PITCH, LANES), F32),
                pltpu.VMEM((blk * n_chunks, LANES), F32),
                pltpu.SemaphoreType.DMA((2,)),
                pltpu.SemaphoreType.DMA,
            ],
        ),
        compiler_params=_params("arbitrary"),
        name="moe_dispatch",
    )(pad_start, pad_cnt, n_used, dest_tiles, x_tok, g.reshape(1, d), shift, scale)


def _expert_kernel(be_ref, nu_ref, ne_ref, x_ref, wgu_hbm, wd_hbm, o_ref,
                   xs, act, wgu_b, wd_b, stage_gu, stage_d, wsem, *, layer, n_chunks):
    i = pl.program_id(0)
    n_used = nu_ref[0]
    blk, d = xs.shape
    d_ff = act.shape[1]
    b = i
    live = i < n_used
    expert = be_ref[b]

    def weight_copies(e):
        return (pltpu.make_async_copy(wgu_hbm.at[layer, e], stage_gu, wsem.at[0]),
                pltpu.make_async_copy(wd_hbm.at[layer, e], stage_d, wsem.at[1]))

    @pl.when(i == 0)
    def _prologue():
        for c in weight_copies(expert):
            c.start()

    @pl.when(live)
    def _live():
        @pl.when(jnp.logical_or(b == 0, expert != be_ref[jnp.maximum(b - 1, 0)]))
        def _new_expert():
            for c in weight_copies(expert):
                c.wait()
            rows_gu = d // 8
            for c in range(8):
                rs = slice(c * rows_gu, (c + 1) * rows_gu)
                wgu_b[rs, :] = stage_gu[rs, :].astype(BF16)
            rows_d = d_ff // 4
            for c in range(4):
                rs = slice(c * rows_d, (c + 1) * rows_d)
                wd_b[rs, :] = stage_d[rs, :].astype(BF16)

            @pl.when(ne_ref[b] != expert)
            def _next_weights():
                for c in weight_copies(ne_ref[b]):
                    c.start()

        for s in range(n_chunks):
            xs[:, s * LANES:(s + 1) * LANES] = _rows_chunk(x_ref, s, blk, n_chunks).astype(xs.dtype)
        wide = 2 * LANES
        for c in range(d_ff // wide):
            a = jnp.dot(xs[...], wgu_b[:, c * wide:(c + 1) * wide], preferred_element_type=F32)
            gate = jnp.dot(xs[...], wgu_b[:, d_ff + c * wide:d_ff + (c + 1) * wide], preferred_element_type=F32)
            act[:, c * wide:(c + 1) * wide] = (_silu(a) * gate).astype(act.dtype)
        for c in range(d // (2 * wide)):
            y = jnp.dot(act[...], wd_b[:, c * 2 * wide:(c + 1) * 2 * wide], preferred_element_type=F32)
            for s in range(4):
                o_ref[pl.ds(4 * c + s, blk, stride=n_chunks), :] = y[:, s * LANES:(s + 1) * LANES]

    @pl.when(jnp.logical_not(live))
    def _idle():
        _zero_rows(o_ref)


def _experts(x_sorted, w_gu, w_down, layer, block_e, n_used, next_e):
    _, n_exp, d, two_ff = w_gu.shape
    d_ff = two_ff // 2
    blk = ROW_TILE
    n_chunks = d // LANES
    n_blocks = block_e.shape[0]
    kern = functools.partial(_expert_kernel, layer=layer, n_chunks=n_chunks)
    return pl.pallas_call(
        kern,
        out_shape=jax.ShapeDtypeStruct((n_blocks * blk * n_chunks, LANES), F32),
        grid_spec=pltpu.PrefetchScalarGridSpec(
            num_scalar_prefetch=3,
            grid=(n_blocks,),
            in_specs=[
                pl.BlockSpec((blk * n_chunks, LANES), lambda i, *_: (i, 0)),
                pl.BlockSpec(memory_space=pl.ANY),
                pl.BlockSpec(memory_space=pl.ANY),
            ],
            out_specs=pl.BlockSpec((blk * n_chunks, LANES), lambda i, *_: (i, 0)),
            scratch_shapes=[
                pltpu.VMEM((blk, d), BF16),
                pltpu.VMEM((blk, d_ff), BF16),
                pltpu.VMEM((d, two_ff), BF16),
                pltpu.VMEM((d_ff, d), BF16),
                pltpu.VMEM((d, two_ff), F32),
                pltpu.VMEM((d_ff, d), F32),
                pltpu.SemaphoreType.DMA((2,)),
            ],
        ),
        compiler_params=_params("arbitrary"),
        name="moe_experts",
    )(block_e, n_used, next_e, x_sorted, w_gu, w_down)


def _combine_kernel(dst0_ref, dstn_ref, ys_hbm, x_ref, gate_ref, g2_ref, ng_ref, nsh_ref, nsc_ref, *refs,
                    n_chunks, final_norm):
    if final_norm:
        o_ref, ybuf, sem = refs
        h_ref = o_ref
    else:
        o_ref, h_ref, ybuf, sem = refs
    i = pl.program_id(0)
    n = pl.num_programs(0)
    blk, d = x_ref.shape
    pitch = VMEM_ROW_PITCH
    slot = lax.rem(i, 2)
    tile_rows = blk * n_chunks

    def gather_start(idx_ref, buf):
        for r in range(blk):
            for k in range(TOP_K):
                src = pl.multiple_of(idx_ref[0, k * blk + r], n_chunks)
                pltpu.make_async_copy(ys_hbm.at[pl.ds(src, n_chunks)],
                                      ybuf.at[buf, k, pl.ds(r * pitch, n_chunks)],
                                      sem.at[buf]).start(priority=k % DMA_PRIORITIES)

    @pl.when(i == 0)
    def _prologue():
        gather_start(dst0_ref, 0)

    @pl.when(i + 1 < n)
    def _prefetch():
        gather_start(dstn_ref, 1 - slot)

    for k in range(TOP_K):
        pltpu.make_async_copy(ys_hbm.at[pl.ds(0, tile_rows)],
                              ybuf.at[slot, k, pl.ds(0, tile_rows)], sem.at[slot]).wait()
    gates = [jnp.broadcast_to(gate_ref[:, k:k + 1], (blk, LANES)) for k in range(TOP_K)]
    sq = jnp.zeros((blk, 1), F32)
    for s in range(n_chunks):
        cs = slice(s * LANES, (s + 1) * LANES)
        f = (_rows_chunk(ybuf.at[slot, 0], s, blk, pitch) * gates[0]
             + _rows_chunk(ybuf.at[slot, 1], s, blk, pitch) * gates[1])
        xo = x_ref[:, cs] + g2_ref[0, :, cs] * f
        o_ref[:, cs] = xo
        sq = sq + jnp.sum(xo * xo, axis=-1, keepdims=True)
    r = lax.rsqrt(sq * (1.0 / d) + NORM_EPS)
    for s in range(n_chunks):
        cs = slice(s * LANES, (s + 1) * LANES)
        hn = (o_ref[:, cs] * r) * ng_ref[:, cs] * (1.0 + nsc_ref[0, :, cs]) + nsh_ref[0, :, cs]
        h_ref[:, cs] = hn.astype(h_ref.dtype)


def _combine(ys, dest_tiles, x_tok, gates, g2, norm_g, norm_shift, norm_scale, n_rows, seq, final_norm):
    assert TOP_K == 2
    d = x_tok.shape[1]
    n_batch = g2.shape[0] - 1
    blk = ROW_TILE
    n_chunks = d // LANES
    tiles = n_rows // blk
    mod = _mod_index(seq // blk, n_batch)
    kern = functools.partial(_combine_kernel, n_chunks=n_chunks, final_norm=final_norm)
    smem_row = lambda index: pl.BlockSpec((None, 1, TOP_K * blk), index, memory_space=pltpu.SMEM)
    row_spec = pl.BlockSpec((blk, d), lambda i: (i, 0))
    if final_norm:
        out_shape, out_specs = jax.ShapeDtypeStruct((n_rows, d), F32), row_spec
    else:
        out_shape = (jax.ShapeDtypeStruct((n_rows, d), F32), jax.ShapeDtypeStruct((n_rows, d), BF16))
        out_specs = (row_spec, row_spec)
    return pl.pallas_call(
        kern,
        out_shape=out_shape,
        grid=(tiles,),
        in_specs=[
            smem_row(lambda i: (0, 0, 0)),
            smem_row(lambda i: (jnp.minimum(i + 1, tiles - 1), 0, 0)),
            pl.BlockSpec(memory_space=pl.ANY),
            pl.BlockSpec((blk, d), lambda i: (i, 0)),
            pl.BlockSpec((blk, TOP_K), lambda i: (i, 0)),
            pl.BlockSpec((1, 1, d), mod),
            pl.BlockSpec((1, d), lambda i: (0, 0)),
            pl.BlockSpec((1, 1, d), mod),
            pl.BlockSpec((1, 1, d), mod),
        ],
        out_specs=out_specs,
        scratch_shapes=[
            pltpu.VMEM((2, TOP_K, blk * VMEM_ROW_PITCH, LANES), F32),
            pltpu.SemaphoreType.DMA((2,)),
        ],
        compiler_params=_params("arbitrary"),
        name="moe_combine",
    )(dest_tiles, dest_tiles, ys, x_tok, gates.T, g2, norm_g.reshape(1, d), norm_shift, norm_scale)


def kernel(x, c, ctx, c_ctx, w_ada, b_ada, norm1_g, norm2_g, w_in, ret_decay, pool_w, pool_scale, sgu_w, sgu_b,
           w_branch, w_out, w_router, router_bias, w_gu, w_down, final_g):
    n_batch, seq, d = x.shape
    ctx_len = ctx.shape[1]
    depth = w_ada.shape[0]
    bw = d // 2
    t_x = n_batch * seq
    t_all = t_x + n_batch * ctx_len
    n_exp = w_router.shape[1]
    assert seq % ROW_TILE == 0 and ctx_len % ROW_TILE == 0 and ctx_len % RET_CHUNK == 0
    assert GRID_W & (GRID_W - 1) == 0 and seq % GRID_W == 0

    pad = (-(n_batch + 1)) % 8
    c_all = jnp.concatenate([c, c_ctx[None, :], jnp.zeros((pad, d), F32)], axis=0)
    ada = _ada(c_all, w_ada, b_ada)
    mods = [[m[:n_batch + 1, None, :] for m in jnp.split(ada[l], 6, axis=-1)] for l in range(depth)]
    no_mod = jnp.zeros((n_batch + 1, 1, d), F32)
    cos_t, sin_t = _rope_tables(n_batch, seq, ctx_len)
    w_branch_b, w_out_b = w_branch.astype(BF16), w_out.astype(BF16)

    x_main, x_ctx = x.reshape(t_x, d), ctx.reshape(n_batch * ctx_len, d)
    h = _norm(x_main, x_ctx, norm1_g[0], mods[0][0], mods[0][1], seq)
    out = None
    for l in range(depth):
        last = l == depth - 1
        sh1, sc1, g1, sh2, sc2, g2 = mods[l]
        rows_mix = t_x if last else t_all
        p = _in_proj(h, w_in, l, cos_t, sin_t, t_all, tm=_lane_tile(2304, t_all), tn=bw)
        lg = jax.nn.log_sigmoid(ret_decay[l].astype(F32)).T
        lg = jnp.broadcast_to(lg[:, :, None], lg.shape + (RET_CHUNK,))
        ret = _retention(p, lg, n_batch, seq, ctx_len)
        pooled, pooled_ctx = _pool(p, pool_w[l].astype(BF16), pool_scale[l], 4 * bw, n_batch, seq, ctx_len, not last)
        merged = _merge(ret, pooled, pooled_ctx, sgu_w[l], sgu_b[l], w_branch_b, l, p, 5 * bw, 7 * bw, rows_mix)
        x_tok, eid, gates = _matmul_residual(merged, w_out_b, l, x_main, x_ctx if rows_mix > t_x else None, g1,
                                             norm2_g[l], sh2, sc2, w_router, router_bias, rows_mix, seq)
        dest_tiles, block_e, n_used, next_e, pad_start, pad_cnt = _route_plan(eid, n_exp, d // LANES)
        x_sorted = _dispatch(x_tok, norm2_g[l], sh2, sc2, dest_tiles, pad_start, pad_cnt, n_used,
                             rows_mix, block_e.shape[0] * ROW_TILE, seq)
        ys = _experts(x_sorted, w_gu, w_down, l, block_e, n_used, next_e)
        if last:
            out = _combine(ys, dest_tiles, x_tok, gates, g2, final_g, no_mod, no_mod, rows_mix, seq, True)
        else:
            x_main, h = _combine(ys, dest_tiles, x_tok, gates, g2, norm1_g[l + 1], mods[l + 1][0], mods[l + 1][1],
                                 rows_mix, seq, False)
            x_ctx = x_main[t_x:]
    return out.reshape(n_batch, seq, d)
```
